```python
import jax, jax.numpy as jnp
from jax import lax
import numpy as np

D_MODEL = 2048
BATCH = 2
SEQ = 8192
DEPTH = 4

CTX_LEN = 256
GRID_W = 64
CHUNK = 128
EPS = 1e-6

A_HEADS = 4
A_DIM = 128
A_WIDTH = A_HEADS * A_DIM
B_HEADS = 8
B_KV_HEADS = 2
HEAD_DIM = 128
B_GROUP = B_HEADS // B_KV_HEADS
B_WIDTH = B_HEADS * HEAD_DIM
B_KV_WIDTH = B_KV_HEADS * HEAD_DIM
WINDOW = 128
ROPE_THETA = 10000.0
ROPE_FREQS = HEAD_DIM // 4
C_GROUPS = 4
C_DIM = 128
C_WIDTH = C_GROUPS * C_DIM

MIX_WIDTH = A_WIDTH + B_WIDTH + C_WIDTH
SPLIT_SIZES = (A_WIDTH, A_WIDTH, A_WIDTH, B_WIDTH, B_KV_WIDTH, B_KV_WIDTH, B_WIDTH, C_WIDTH, C_WIDTH)
IN_COLS = 3 * A_WIDTH + 2 * B_WIDTH + 2 * B_KV_WIDTH + 2 * C_WIDTH
NEG_INF = -1e30

kernel_name = "hybrid_gmlp_swa_fnet_diffusion_trunk"


def rmsnorm(x, g):
    xf = x.astype(jnp.float32)
    y = xf * lax.rsqrt(jnp.mean(xf * xf, axis=-1, keepdims=True) + EPS)
    return (y * g.astype(jnp.float32)).astype(x.dtype)


def split_cols(proj):
    idx, acc = [], 0
    for s in SPLIT_SIZES[:-1]:
        acc += s
        idx.append(acc)
    return jnp.split(proj, idx, axis=-1)


def rope_tables(row, col):
    freqs = ROPE_THETA ** (-jnp.arange(ROPE_FREQS, dtype=jnp.float32) / ROPE_FREQS)
    ang = jnp.stack([row.astype(jnp.float32)[:, None] * freqs,
                     col.astype(jnp.float32)[:, None] * freqs], axis=1)
    ang = jnp.broadcast_to(ang[:, :, None, :], (ang.shape[0], 2, 2, ROPE_FREQS))
    ang = ang.reshape(ang.shape[0], HEAD_DIM)
    return jnp.cos(ang), jnp.sin(ang)


def rope_2d(x, cos, sin):
    shp = x.shape
    xf = x.astype(jnp.float32).reshape(*shp[:-1], 2, 2, ROPE_FREQS)
    rot = jnp.stack([-xf[..., 1, :], xf[..., 0, :]], axis=-2).reshape(shp)
    out = xf.reshape(shp) * cos[None, :, None, :] + rot * sin[None, :, None, :]
    return out.astype(x.dtype)


def chunk_gmlp(u, v, g_sgu, w_s, b_s):
    bsz, t, _ = v.shape
    nc = t // CHUNK
    vn = rmsnorm(v, g_sgu).reshape(bsz, nc, CHUNK, A_HEADS, A_DIM)
    mixed = jnp.einsum('hpq,bcqhd->bcphd', w_s, vn) + b_s.T[None, None, :, :, None]
    return u * mixed.reshape(bsz, t, A_WIDTH)


def fourier_mix(xc, w_f, b_f):
    bsz, t, _ = xc.shape
    xg = xc.reshape(bsz, t, C_GROUPS, C_DIM).astype(jnp.float32)
    y = jnp.fft.fft2(xg, axes=(1, 3), norm='ortho').real.astype(xc.dtype)
    y = jnp.einsum('btgc,gcd->btgd', y, w_f) + b_f
    return y.reshape(bsz, t, C_WIDTH)


def sink_column(sink, shape_prefix):
    s = sink.astype(jnp.float32).reshape(B_KV_HEADS, B_GROUP)
    return jnp.broadcast_to(s[:, :, None, None], shape_prefix + (1,))


def window_attention(q, k, v, k_ctx, v_ctx, sink):
    bsz, s, _, dh = q.shape
    nb = s // CHUNK
    scale = dh ** -0.5
    pad = ((0, 0), (CHUNK, CHUNK), (0, 0), (0, 0))
    kp = jnp.pad(k, pad).reshape(bsz, nb + 2, CHUNK, B_KV_HEADS, dh)
    vp = jnp.pad(v, pad).reshape(bsz, nb + 2, CHUNK, B_KV_HEADS, dh)
    kw = jnp.concatenate([kp[:, :-2], kp[:, 1:-1], kp[:, 2:]], axis=2)
    vw = jnp.concatenate([vp[:, :-2], vp[:, 1:-1], vp[:, 2:]], axis=2)
    qb = q.reshape(bsz, nb, CHUNK, B_KV_HEADS, B_GROUP, dh)
    s_loc = jnp.einsum('bnqkgd,bnjkd->bnkgqj', qb, kw).astype(jnp.float32) * scale
    s_ctx = jnp.einsum('bnqkgd,bmkd->bnkgqm', qb, k_ctx).astype(jnp.float32) * scale
    a = jnp.arange(CHUNK)[:, None]
    j = jnp.arange(3 * CHUNK)[None, :]
    blk = jnp.arange(nb)[:, None, None]
    key_pos = blk * CHUNK - CHUNK + j
    mask = (jnp.abs(j - CHUNK - a) <= WINDOW)[None] & (key_pos >= 0) & (key_pos < s)
    s_loc = jnp.where(mask[None, :, None, None], s_loc, NEG_INF)
    sk = sink_column(sink, (bsz, nb, B_KV_HEADS, B_GROUP, CHUNK))
    n_ctx = k_ctx.shape[1]
    p = jax.nn.softmax(jnp.concatenate([sk, s_ctx, s_loc], axis=-1), axis=-1)
    p_ctx = p[..., 1:1 + n_ctx].astype(v.dtype)
    p_loc = p[..., 1 + n_ctx:].astype(v.dtype)
    o = (jnp.einsum('bnkgqm,bmkd->bnqkgd', p_ctx, v_ctx)
         + jnp.einsum('bnkgqj,bnjkd->bnqkgd', p_loc, vw))
    return o.reshape(bsz, s, B_WIDTH)


def context_attention(q, k, v, sink):
    bsz, n, _, dh = q.shape
    qg = q.reshape(bsz, n, B_KV_HEADS, B_GROUP, dh)
    s = jnp.einsum('blkgd,bmkd->bkglm', qg, k).astype(jnp.float32) * (dh ** -0.5)
    sk = sink_column(sink, (bsz, B_KV_HEADS, B_GROUP, n))
    p = jax.nn.softmax(jnp.concatenate([sk, s], axis=-1), axis=-1)[..., 1:].astype(v.dtype)
    o = jnp.einsum('bkglm,bmkd->blkgd', p, v)
    return o.reshape(bsz, n, B_WIDTH)


def mixer_branches(proj, attn_fn, g_sgu, w_s, b_s, w_f, b_f):
    a_u, a_v, a_g, b_q, b_k, b_v, b_g, c_x, c_g = split_cols(proj)
    y_a = chunk_gmlp(jax.nn.gelu(a_u), jax.nn.gelu(a_v), g_sgu, w_s, b_s) * jax.nn.silu(a_g)
    y_b = attn_fn(b_q, b_k, b_v) * jax.nn.silu(b_g)
    y_c = fourier_mix(c_x, w_f, b_f) * jax.nn.silu(c_g)
    return jnp.concatenate([y_a, y_b, y_c], axis=-1)


def heads(t, n_heads):
    return t.reshape(t.shape[0], t.shape[1], n_heads, HEAD_DIM)


def setup_inputs(seed: int = 0) -> dict:
    key = jax.random.key(seed)
    ks = jax.random.split(key, 20)
    nrm = jax.random.normal
    f32 = jnp.float32
    return {
        "x": nrm(ks[0], (BATCH, SEQ, D_MODEL), f32),
        "c": nrm(ks[1], (BATCH, D_MODEL), f32),
        "ctx": nrm(ks[2], (BATCH, CTX_LEN, D_MODEL), f32),
        "c_ctx": nrm(ks[3], (D_MODEL,), f32),
        "w_mod": nrm(ks[4], (DEPTH, D_MODEL, 3 * D_MODEL), f32) * D_MODEL ** -0.5,
        "b_mod": nrm(ks[5], (DEPTH, 3 * D_MODEL), f32) * 0.01,
        "g_pre": 1.0 + 0.01 * nrm(ks[6], (DEPTH, D_MODEL), f32),
        "g_post": 1.0 + 0.01 * nrm(ks[7], (DEPTH, D_MODEL), f32),
        "w_in": nrm(ks[8], (DEPTH, D_MODEL, IN_COLS), f32) * D_MODEL ** -0.5,
        "w_out": nrm(ks[9], (DEPTH, MIX_WIDTH, D_MODEL), f32) * MIX_WIDTH ** -0.5,
        "g_sgu": 1.0 + 0.01 * nrm(ks[10], (DEPTH, A_WIDTH), f32),
        "w_sgu": nrm(ks[11], (DEPTH, A_HEADS, CHUNK, CHUNK), f32) * CHUNK ** -0.5,
        "b_sgu": nrm(ks[12], (DEPTH, A_HEADS, CHUNK), f32) * 0.01,
        "sink": nrm(ks[13], (DEPTH, B_HEADS), f32) * 0.5,
        "w_fourier": nrm(ks[14], (DEPTH, C_GROUPS, C_DIM, C_DIM), f32) * C_DIM ** -0.5,
        "b_fourier": nrm(ks[15], (DEPTH, C_GROUPS, C_DIM), f32) * 0.01,
    }


def reference(x, c, ctx, c_ctx, w_mod, b_mod, g_pre, g_post, w_in, w_out,
              g_sgu, w_sgu, b_sgu, sink, w_fourier, b_fourier):
    s = x.shape[1]
    rows = s // GRID_W
    row = jnp.repeat(jnp.arange(rows), GRID_W)
    col = jnp.tile(jnp.arange(GRID_W), rows)
    cos, sin = rope_tables(row, col)
    silu_c = jax.nn.silu(c)
    silu_cc = jax.nn.silu(c_ctx)

    for l in range(DEPTH):
        last = l == DEPTH - 1
        shift, scale, gate = jnp.split(silu_c @ w_mod[l] + b_mod[l], 3, axis=-1)
        shift_c, scale_c, gate_c = jnp.split(silu_cc @ w_mod[l] + b_mod[l], 3, axis=-1)

        h = rmsnorm(x, g_pre[l]) * (1.0 + scale[:, None]) + shift[:, None]
        h_c = rmsnorm(ctx, g_pre[l]) * (1.0 + scale_c) + shift_c
        proj = h @ w_in[l]
        proj_c = h_c @ w_in[l]

        pc = split_cols(proj_c)
        k_ctx = heads(pc[4], B_KV_HEADS)
        v_ctx = heads(pc[5], B_KV_HEADS)
        sink_l = sink[l]

        def latent_attn(bq, bk, bv):
            q = rope_2d(heads(bq, B_HEADS), cos, sin)
            k = rope_2d(heads(bk, B_KV_HEADS), cos, sin)
            return window_attention(q, k, heads(bv, B_KV_HEADS), k_ctx, v_ctx, sink_l)

        y = mixer_branches(proj, latent_attn, g_sgu[l], w_sgu[l], b_sgu[l],
                           w_fourier[l], b_fourier[l])
        y = rmsnorm(y @ w_out[l], g_post[l])
        x_new = x + gate[:, None] * y

        if not last:
            def ctx_attn(bq, bk, bv):
                return context_attention(heads(bq, B_HEADS), heads(bk, B_KV_HEADS),
                                         heads(bv, B_KV_HEADS), sink_l)

            y_c = mixer_branches(proj_c, ctx_attn, g_sgu[l], w_sgu[l], b_sgu[l],
                                 w_fourier[l], b_fourier[l])
            y_c = rmsnorm(y_c @ w_out[l], g_post[l])
            ctx = ctx + gate_c * y_c
        x = x_new
    return x
```

```python
import functools
import math

import numpy as np
import jax
import jax.numpy as jnp
from jax.experimental import pallas as pl
from jax.experimental.pallas import tpu as pltpu

F32 = jnp.float32
BF16 = jnp.bfloat16

EPS = 1e-6
GRID_W = 64
CHUNK = 128
A_HEADS = 4
A_WIDTH = 512
B_HEADS = 8
B_KV_HEADS = 2
B_GROUP = B_HEADS // B_KV_HEADS
HEAD_DIM = 128
B_WIDTH = 1024
B_KV_WIDTH = 256
ROPE_THETA = 10000.0
ROPE_FREQS = HEAD_DIM // 4
C_GROUPS = 4
C_DIM = 128
C_WIDTH = 512
IN_COLS = 5120
NEG_INF = -1e30
ATTN_SCALE = HEAD_DIM ** -0.5

TN = 512
COL_AU, COL_AV, COL_AG, COL_Q0, COL_Q1, COL_KV, COL_BG0, COL_BG1, COL_CX, COL_CG = range(10)
N_COL_TILES = IN_COLS // TN
FFT_S1 = 64

VMEM_LIMIT = 56 * 1024 * 1024


def _cparams(sem):
    return pltpu.CompilerParams(dimension_semantics=sem, vmem_limit_bytes=VMEM_LIMIT)


def _silu(x):
    return x / (1.0 + jnp.exp(-x))


def _gelu_tanh(x):
    c = math.sqrt(2.0 / math.pi)
    return x * (0.5 * (1.0 + jnp.tanh(c * (x + 0.044715 * (x * x * x)))))


def _mod_kernel(cc_ref, w_ref, b_ref, o_ref):
    s = _silu(cc_ref[...]).astype(BF16)
    w = w_ref[...].astype(BF16)
    o_ref[...] = jnp.dot(s, w, preferred_element_type=F32) + b_ref[...]


def _modulation(cc, w_mod, b_mod):
    depth, d, n3 = w_mod.shape
    rows = cc.shape[0]
    tn = 768
    return pl.pallas_call(
        _mod_kernel,
        grid=(depth, n3 // tn),
        in_specs=[
            pl.BlockSpec((rows, d), lambda l, j: (0, 0)),
            pl.BlockSpec((None, d, tn), lambda l, j: (l, 0, j)),
            pl.BlockSpec((None, 1, tn), lambda l, j: (l, 0, j)),
        ],
        out_specs=pl.BlockSpec((None, rows, tn), lambda l, j: (l, 0, j)),
        out_shape=jax.ShapeDtypeStruct((depth, rows, n3), F32),
        compiler_params=_cparams(("parallel", "parallel")),
        name="modulation",
    )(cc, w_mod, b_mod.reshape(depth, 1, n3))


def _rope_heads(t, cos, sin_a, sin_b, n_heads, scale):
    outs = []
    for h in range(n_heads):
        xh = t[:, h * HEAD_DIM:(h + 1) * HEAD_DIM]
        up = pltpu.roll(xh, HEAD_DIM - ROPE_FREQS, axis=1)
        dn = pltpu.roll(xh, ROPE_FREQS, axis=1)
        o = xh * cos + up * sin_a + dn * sin_b
        if scale != 1.0:
            o = o * scale
        outs.append(o)
    return outs


def _inproj_kernel(*refs, use_rope):
    if use_rope:
        (x_ref, shift_ref, scale_ref, gpre_ref, w_ref, gsgu_ref, cos_ref, sina_ref, sinb_ref,
         proj_ref, cx_ref, h_ref) = refs
    else:
        (x_ref, shift_ref, scale_ref, gpre_ref, w_ref, gsgu_ref,
         proj_ref, cx_ref, h_ref) = refs
    j = pl.program_id(1)

    @pl.when(j == 0)
    def _():
        x = x_ref[...]
        ms = jnp.mean(x * x, axis=-1, keepdims=True)
        y = x * jax.lax.rsqrt(ms + EPS) * gpre_ref[...]
        h_ref[...] = (y * (1.0 + scale_ref[...]) + shift_ref[...]).astype(BF16)

    acc = jnp.dot(h_ref[...], w_ref[...], preferred_element_type=F32)

    @pl.when(j == COL_AU)
    def _():
        proj_ref[...] = _gelu_tanh(acc).astype(BF16)

    @pl.when(j == COL_AV)
    def _():
        v = _gelu_tanh(acc)
        ms = jnp.mean(v * v, axis=-1, keepdims=True)
        proj_ref[...] = (v * jax.lax.rsqrt(ms + EPS) * gsgu_ref[...]).astype(BF16)

    @pl.when((j == COL_AG) | (j == COL_BG0) | (j == COL_BG1) | (j == COL_CG))
    def _():
        proj_ref[...] = _silu(acc).astype(BF16)

    @pl.when((j == COL_Q0) | (j == COL_Q1))
    def _():
        if use_rope:
            outs = _rope_heads(acc, cos_ref[...], sina_ref[...], sinb_ref[...], TN // HEAD_DIM, ATTN_SCALE)
            for h, o in enumerate(outs):
                proj_ref[:, h * HEAD_DIM:(h + 1) * HEAD_DIM] = o.astype(BF16)
        else:
            proj_ref[...] = (acc * ATTN_SCALE).astype(BF16)

    @pl.when(j == COL_KV)
    def _():
        if use_rope:
            outs = _rope_heads(acc[:, :B_KV_WIDTH], cos_ref[...], sina_ref[...], sinb_ref[...], B_KV_HEADS, 1.0)
            for h, o in enumerate(outs):
                proj_ref[:, h * HEAD_DIM:(h + 1) * HEAD_DIM] = o.astype(BF16)
            proj_ref[:, B_KV_WIDTH:] = acc[:, B_KV_WIDTH:].astype(BF16)
        else:
            proj_ref[...] = acc.astype(BF16)

    @pl.when(j == COL_CX)
    def _():
        proj_ref[...] = acc.astype(BF16)
        cx_ref[...] = acc


def _inproj(x2, shift, scale, g_pre, w_in, g_sgu, rope, rows_per_mod, tm):
    r, d = x2.shape
    use_rope = rope is not None
    tiles_per_mod = rows_per_mod // tm
    in_specs = [
        pl.BlockSpec((tm, d), lambda i, j: (i, 0)),
        pl.BlockSpec((None, 1, d), lambda i, j: (i // tiles_per_mod, 0, 0)),
        pl.BlockSpec((None, 1, d), lambda i, j: (i // tiles_per_mod, 0, 0)),
        pl.BlockSpec((1, d), lambda i, j: (0, 0)),
        pl.BlockSpec((d, TN), lambda i, j: (0, j)),
        pl.BlockSpec((1, A_WIDTH), lambda i, j: (0, 0)),
    ]
    args = [x2, shift, scale, g_pre, w_in, g_sgu]
    if use_rope:
        seq = rope[0].shape[0]
        tiles_per_seq = seq // tm
        for t in rope:
            in_specs.append(pl.BlockSpec((tm, HEAD_DIM), lambda i, j: (i % tiles_per_seq, 0)))
            args.append(t)
    return pl.pallas_call(
        functools.partial(_inproj_kernel, use_rope=use_rope),
        grid=(r // tm, N_COL_TILES),
        in_specs=in_specs,
        out_specs=[
            pl.BlockSpec((tm, TN), lambda i, j: (i, j)),
            pl.BlockSpec((tm, C_WIDTH), lambda i, j: (i, 0)),
        ],
        out_shape=[
            jax.ShapeDtypeStruct((r, IN_COLS), BF16),
            jax.ShapeDtypeStruct((r, C_WIDTH), F32),
        ],
        scratch_shapes=[pltpu.VMEM((tm, d), BF16)],
        compiler_params=_cparams(("parallel", "arbitrary")),
        name="inproj_rope" if use_rope else "inproj_ctx",
    )(*args)


def _gmlp_kernel(gu_ref, vn_ref, sg_ref, ws_ref, bias_ref, o_ref, *, n_chunks):
    for c in range(n_chunks):
        rows = slice(c * CHUNK, (c + 1) * CHUNK)
        for h in range(A_HEADS):
            cols = slice(h * HEAD_DIM, (h + 1) * HEAD_DIM)
            mixed = jnp.dot(ws_ref[h], vn_ref[rows, cols], preferred_element_type=F32)
            mixed = mixed + bias_ref[:, cols]
            y = gu_ref[rows, cols].astype(F32) * mixed * sg_ref[rows, cols].astype(F32)
            o_ref[rows, cols] = y.astype(BF16)


def _gmlp(proj, w_s, bias_full, tg):
    r = proj.shape[0]
    return pl.pallas_call(
        functools.partial(_gmlp_kernel, n_chunks=tg // CHUNK),
        grid=(r // tg,),
        in_specs=[
            pl.BlockSpec((tg, A_WIDTH), lambda i: (i, COL_AU)),
            pl.BlockSpec((tg, A_WIDTH), lambda i: (i, COL_AV)),
            pl.BlockSpec((tg, A_WIDTH), lambda i: (i, COL_AG)),
            pl.BlockSpec((A_HEADS, CHUNK, CHUNK), lambda i: (0, 0, 0)),
            pl.BlockSpec((CHUNK, A_WIDTH), lambda i: (0, 0)),
        ],
        out_specs=pl.BlockSpec((tg, A_WIDTH), lambda i: (i, 0)),
        out_shape=jax.ShapeDtypeStruct((r, A_WIDTH), BF16),
        compiler_params=_cparams(("parallel",)),
        name="gmlp",
    )(proj, proj, proj, w_s, bias_full)


def _dot_nt(a, b):
    return jax.lax.dot_general(a, b, (((1,), (1,)), ((), ())), preferred_element_type=F32)


def _attn_block(q, keys, vals, masks, sink_val):
    scores = []
    for k, m in zip(keys, masks):
        s = _dot_nt(q, k)
        if m is not None:
            s = jnp.where(m, s, NEG_INF)
        scores.append(s)
    mx = scores[0].max(axis=-1, keepdims=True)
    for s in scores[1:]:
        mx = jnp.maximum(mx, s.max(axis=-1, keepdims=True))
    mx = jnp.maximum(mx, sink_val)
    den = jnp.exp(sink_val - mx)
    o = None
    for s, v in zip(scores, vals):
        p = jnp.exp(s - mx)
        den = den + p.sum(axis=-1, keepdims=True)
        pv = jnp.dot(p.astype(BF16), v, preferred_element_type=F32)
        o = pv if o is None else o + pv
    return o / den


def _win_attn_kernel(sink_ref, q_ref, kp_ref, km_ref, kn_ref, vp_ref, vm_ref, vn_ref,
                     kc_ref, vc_ref, g_ref, o_ref, *, nq):
    i = pl.program_id(1)
    kv = pl.program_id(2)
    row = jax.lax.broadcasted_iota(jnp.int32, (CHUNK, CHUNK), 0)
    col = jax.lax.broadcasted_iota(jnp.int32, (CHUNK, CHUNK), 1)
    not_first = i > 0
    not_last = i < pl.num_programs(1) - 1
    mask_prev = col >= row
    mask_next = col <= row
    for a in range(nq):
        rows = slice(a * CHUNK, (a + 1) * CHUNK)
        if a == 0:
            k_prev, v_prev = kp_ref[...], vp_ref[...]
            m_prev = mask_prev & not_first
        else:
            k_prev, v_prev = km_ref[(a - 1) * CHUNK:a * CHUNK, :], vm_ref[(a - 1) * CHUNK:a * CHUNK, :]
            m_prev = mask_prev
        if a == nq - 1:
            k_next, v_next = kn_ref[...], vn_ref[...]
            m_next = mask_next & not_last
        else:
            k_next, v_next = km_ref[(a + 1) * CHUNK:(a + 2) * CHUNK, :], vm_ref[(a + 1) * CHUNK:(a + 2) * CHUNK, :]
            m_next = mask_next
        keys = [kc_ref[...], k_prev, km_ref[rows, :], k_next]
        vals = [vc_ref[...], v_prev, vm_ref[rows, :], v_next]
        masks = [None, m_prev, None, m_next]
        for g in range(B_GROUP):
            cols = slice(g * HEAD_DIM, (g + 1) * HEAD_DIM)
            sink_val = sink_ref[kv * B_GROUP + g]
            o = _attn_block(q_ref[rows, cols], keys, vals, masks, sink_val)
            o_ref[rows, cols] = (o * g_ref[rows, cols].astype(F32)).astype(BF16)


def _win_attn(proj3, projc3, sink_l, nq):
    bsz, s, _ = proj3.shape
    n_ctx = projc3.shape[1]
    tq = nq * CHUNK
    nb = s // CHUNK
    kcol = (COL_KV * TN) // HEAD_DIM
    vcol = kcol + B_KV_HEADS
    smem = pl.BlockSpec(memory_space=pltpu.SMEM)

    def halo(col0, which):
        if which == "prev":
            return pl.BlockSpec((None, CHUNK, HEAD_DIM),
                                lambda b, i, kv: (b, jnp.maximum(i * nq - 1, 0), col0 + kv))
        if which == "next":
            return pl.BlockSpec((None, CHUNK, HEAD_DIM),
                                lambda b, i, kv: (b, jnp.minimum(i * nq + nq, nb - 1), col0 + kv))
        return pl.BlockSpec((None, tq, HEAD_DIM), lambda b, i, kv: (b, i, col0 + kv))

    return pl.pallas_call(
        functools.partial(_win_attn_kernel, nq=nq),
        grid=(bsz, s // tq, B_KV_HEADS),
        in_specs=[
            smem,
            pl.BlockSpec((None, tq, TN), lambda b, i, kv: (b, i, COL_Q0 + kv)),
            halo(kcol, "prev"), halo(kcol, "main"), halo(kcol, "next"),
            halo(vcol, "prev"), halo(vcol, "main"), halo(vcol, "next"),
            pl.BlockSpec((None, n_ctx, HEAD_DIM), lambda b, i, kv: (b, 0, kcol + kv)),
            pl.BlockSpec((None, n_ctx, HEAD_DIM), lambda b, i, kv: (b, 0, vcol + kv)),
            pl.BlockSpec((None, tq, TN), lambda b, i, kv: (b, i, COL_BG0 + kv)),
        ],
        out_specs=pl.BlockSpec((None, tq, TN), lambda b, i, kv: (b, i, kv)),
        out_shape=jax.ShapeDtypeStruct((bsz, s, B_WIDTH), BF16),
        compiler_params=_cparams(("parallel", "parallel", "parallel")),
        name="win_attn",
    )(sink_l, proj3, proj3, proj3, proj3, proj3, proj3, proj3, projc3, projc3, proj3)


def _ctx_attn_kernel(sink_ref, q_ref, k_ref, v_ref, g_ref, o_ref, *, n_blocks):
    kv = pl.program_id(1)
    for a in range(n_blocks):
        rows = slice(a * CHUNK, (a + 1) * CHUNK)
        for g in range(B_GROUP):
            cols = slice(g * HEAD_DIM, (g + 1) * HEAD_DIM)
            sink_val = sink_ref[kv * B_GROUP + g]
            o = _attn_block(q_ref[rows, cols], [k_ref[...]], [v_ref[...]], [None], sink_val)
            o_ref[rows, cols] = (o * g_ref[rows, cols].astype(F32)).astype(BF16)


def _ctx_attn(projc3, sink_l):
    bsz, n_ctx, _ = projc3.shape
    kcol = (COL_KV * TN) // HEAD_DIM
    vcol = kcol + B_KV_HEADS
    return pl.pallas_call(
        functools.partial(_ctx_attn_kernel, n_blocks=n_ctx // CHUNK),
        grid=(bsz, B_KV_HEADS),
        in_specs=[
            pl.BlockSpec(memory_space=pltpu.SMEM),
            pl.BlockSpec((None, n_ctx, TN), lambda b, kv: (b, 0, COL_Q0 + kv)),
            pl.BlockSpec((None, n_ctx, HEAD_DIM), lambda b, kv: (b, 0, kcol + kv)),
            pl.BlockSpec((None, n_ctx, HEAD_DIM), lambda b, kv: (b, 0, vcol + kv)),
            pl.BlockSpec((None, n_ctx, TN), lambda b, kv: (b, 0, COL_BG0 + kv)),
        ],
        out_specs=pl.BlockSpec((None, n_ctx, TN), lambda b, kv: (b, 0, kv)),
        out_shape=jax.ShapeDtypeStruct((bsz, n_ctx, B_WIDTH), BF16),
        compiler_params=_cparams(("parallel", "parallel")),
        name="ctx_attn",
    )(sink_l, projc3, projc3, projc3, projc3)


def _dft_cos_sin(n, scale):
    k = np.arange(n, dtype=np.int64)
    ang = 2.0 * np.pi * ((k[:, None] * k[None, :]) % n).astype(np.float64) / n
    return np.cos(ang) * scale, np.sin(ang) * scale


def _pos_dft_kernel(x_ref, f_ref, o_ref, *, t):
    z = jnp.dot(f_ref[...], x_ref[...].astype(BF16), preferred_element_type=F32)
    o_ref[0] = z[:t].astype(BF16)
    o_ref[1] = z[t:].astype(BF16)


def _pos_dft_single(cx3):
    bsz, t, m = cx3.shape
    c, s = _dft_cos_sin(t, t ** -0.5)
    f = jnp.asarray(np.concatenate([c, -s], axis=0), dtype=F32).astype(BF16)
    return pl.pallas_call(
        functools.partial(_pos_dft_kernel, t=t),
        grid=(bsz,),
        in_specs=[
            pl.BlockSpec((None, t, m), lambda b: (b, 0, 0)),
            pl.BlockSpec((2 * t, t), lambda b: (0, 0)),
        ],
        out_specs=pl.BlockSpec((None, 2, t, m), lambda b: (b, 0, 0, 0)),
        out_shape=jax.ShapeDtypeStruct((bsz, 2, t, m), BF16),
        compiler_params=_cparams(("parallel",)),
        name="pos_dft_single",
    )(cx3, f)


def _fft_a_kernel(x_ref, f_ref, twr_ref, twi_ref, o_ref, *, s2, tb):
    reps = C_WIDTH // HEAD_DIM
    for j in range(tb):
        xj = x_ref[:, j, :].astype(BF16)
        z = jnp.dot(f_ref[...], xj, preferred_element_type=F32)
        zr, zi = z[:s2], z[s2:]
        twr = jnp.concatenate([twr_ref[j]] * reps, axis=1)
        twi = jnp.concatenate([twi_ref[j]] * reps, axis=1)
        o_ref[0, j] = (zr * twr - zi * twi).astype(BF16)
        o_ref[1, j] = (zr * twi + zi * twr).astype(BF16)


def _fft_b_kernel(z_ref, g_ref, o_ref):
    o_ref[...] = jnp.dot(g_ref[...], z_ref[...], preferred_element_type=F32).astype(BF16)


def _pos_dft_two_stage(cx3):
    bsz, s, m = cx3.shape
    s1 = FFT_S1
    s2 = s // s1
    tb = 8
    c2, sn2 = _dft_cos_sin(s2, s2 ** -0.5)
    f2 = jnp.asarray(np.concatenate([c2, -sn2], axis=0), dtype=F32).astype(BF16)
    c1, sn1 = _dft_cos_sin(s1, s1 ** -0.5)
    g = jnp.asarray(np.block([[c1, sn1], [-sn1, c1]]), dtype=F32).astype(BF16)
    t1 = np.arange(s1, dtype=np.int64)
    k2 = np.arange(s2, dtype=np.int64)
    ang = 2.0 * np.pi * ((t1[:, None] * k2[None, :]) % s).astype(np.float64) / s
    twr = jnp.asarray(np.repeat(np.cos(ang)[:, :, None], HEAD_DIM, axis=2), dtype=F32)
    twi = jnp.asarray(np.repeat(-np.sin(ang)[:, :, None], HEAD_DIM, axis=2), dtype=F32)

    x4 = cx3.reshape(bsz, s2, s1, m)
    za = pl.pallas_call(
        functools.partial(_fft_a_kernel, s2=s2, tb=tb),
        grid=(bsz, s1 // tb),
        in_specs=[
            pl.BlockSpec((None, s2, tb, m), lambda b, i: (b, 0, i, 0)),
            pl.BlockSpec((2 * s2, s2), lambda b, i: (0, 0)),
            pl.BlockSpec((tb, s2, HEAD_DIM), lambda b, i: (i, 0, 0)),
            pl.BlockSpec((tb, s2, HEAD_DIM), lambda b, i: (i, 0, 0)),
        ],
        out_specs=pl.BlockSpec((None, 2, tb, s2, m), lambda b, i: (b, 0, i, 0, 0)),
        out_shape=jax.ShapeDtypeStruct((bsz, 2, s1, s2, m), BF16),
        compiler_params=_cparams(("parallel", "parallel")),
        name="fft_stage_a",
    )(x4, f2, twr, twi)

    zt = za.reshape(bsz, 2 * s1, s2 * m)
    tc = min(4096, s2 * m)
    y = pl.pallas_call(
        _fft_b_kernel,
        grid=(bsz, (s2 * m) // tc),
        in_specs=[
            pl.BlockSpec((None, 2 * s1, tc), lambda b, i: (b, 0, i)),
            pl.BlockSpec((2 * s1, 2 * s1), lambda b, i: (0, 0)),
        ],
        out_specs=pl.BlockSpec((None, 2 * s1, tc), lambda b, i: (b, 0, i)),
        out_shape=jax.ShapeDtypeStruct((bsz, 2 * s1, s2 * m), BF16),
        compiler_params=_cparams(("parallel", "parallel")),
        name="fft_stage_b",
    )(zt, g)
    return y.reshape(bsz, 2, s, m)


def _chan_kernel(y_ref, cs_ref, wf_ref, bf_ref, g_ref, o_ref):
    for grp in range(C_GROUPS):
        cols = slice(grp * C_DIM, (grp + 1) * C_DIM)
        lhs = jnp.concatenate([y_ref[0, :, cols], y_ref[1, :, cols]], axis=1)
        re = jnp.dot(lhs, cs_ref[...], preferred_element_type=F32)
        out = jnp.dot(re.astype(BF16), wf_ref[grp], preferred_element_type=F32) + bf_ref[:, cols]
        o_ref[:, cols] = (out * g_ref[:, cols].astype(F32)).astype(BF16)


def _chan_mix(ypos, proj3, w_f, b_f, tr):
    bsz, _, t, m = ypos.shape
    c, s = _dft_cos_sin(C_DIM, C_DIM ** -0.5)
    cs = jnp.asarray(np.concatenate([c, s], axis=0), dtype=F32).astype(BF16)
    return pl.pallas_call(
        _chan_kernel,
        grid=(bsz, t // tr),
        in_specs=[
            pl.BlockSpec((None, 2, tr, m), lambda b, i: (b, 0, i, 0)),
            pl.BlockSpec((2 * C_DIM, C_DIM), lambda b, i: (0, 0)),
            pl.BlockSpec((C_GROUPS, C_DIM, C_DIM), lambda b, i: (0, 0, 0)),
            pl.BlockSpec((1, m), lambda b, i: (0, 0)),
            pl.BlockSpec((None, tr, TN), lambda b, i: (b, i, COL_CG)),
        ],
        out_specs=pl.BlockSpec((None, tr, m), lambda b, i: (b, i, 0)),
        out_shape=jax.ShapeDtypeStruct((bsz, t, m), BF16),
        compiler_params=_cparams(("parallel", "parallel")),
        name="chan_mix",
    )(ypos, cs, w_f, b_f, proj3)


def _outproj_kernel(ya_ref, yb_ref, yc_ref, w_ref, gpost_ref, gate_ref, x_ref, o_ref):
    acc = jnp.dot(ya_ref[...], w_ref[0:A_WIDTH, :], preferred_element_type=F32)
    acc = acc + jnp.dot(yb_ref[...], w_ref[A_WIDTH:A_WIDTH + B_WIDTH, :], preferred_element_type=F32)
    acc = acc + jnp.dot(yc_ref[...], w_ref[A_WIDTH + B_WIDTH:, :], preferred_element_type=F32)
    ms = jnp.mean(acc * acc, axis=-1, keepdims=True)
    y = acc * jax.lax.rsqrt(ms + EPS) * gpost_ref[...]
    o_ref[...] = x_ref[...] + gate_ref[...] * y


def _outproj(ya, yb, yc, w_out, g_post, gate, x2, rows_per_mod, tm):
    r, d = x2.shape
    tiles_per_mod = rows_per_mod // tm
    mixw = w_out.shape[0]
    return pl.pallas_call(
        _outproj_kernel,
        grid=(r // tm,),
        in_specs=[
            pl.BlockSpec((tm, A_WIDTH), lambda i: (i, 0)),
            pl.BlockSpec((tm, B_WIDTH), lambda i: (i, 0)),
            pl.BlockSpec((tm, C_WIDTH), lambda i: (i, 0)),
            pl.BlockSpec((mixw, d), lambda i: (0, 0)),
            pl.BlockSpec((1, d), lambda i: (0, 0)),
            pl.BlockSpec((None, 1, d), lambda i: (i // tiles_per_mod, 0, 0)),
            pl.BlockSpec((tm, d), lambda i: (i, 0)),
        ],
        out_specs=pl.BlockSpec((tm, d), lambda i: (i, 0)),
        out_shape=jax.ShapeDtypeStruct((r, d), F32),
        compiler_params=_cparams(("parallel",)),
        name="outproj",
    )(ya, yb, yc, w_out, g_post, gate, x2)


def _rope_tables(seq):
    t = np.arange(seq)
    row = (t // GRID_W).astype(np.float64)
    col = (t % GRID_W).astype(np.float64)
    freqs = ROPE_THETA ** (-np.arange(ROPE_FREQS, dtype=np.float64) / ROPE_FREQS)
    ang_r = row[:, None] * freqs[None, :]
    ang_c = col[:, None] * freqs[None, :]
    ang = np.concatenate([ang_r, ang_r, ang_c, ang_c], axis=1)
    cos, sin = np.cos(ang), np.sin(ang)
    first_half = (np.arange(HEAD_DIM) % (2 * ROPE_FREQS)) < ROPE_FREQS
    sin_a = np.where(first_half[None, :], -sin, 0.0)
    sin_b = np.where(first_half[None, :], 0.0, sin)
    return tuple(jnp.asarray(a, dtype=F32) for a in (cos, sin_a, sin_b))


def _pick_tile(n, pref):
    t = min(pref, n)
    while n % t:
        t //= 2
    return t


def kernel(x, c, ctx, c_ctx, w_mod, b_mod, g_pre, g_post, w_in, w_out, g_sgu, w_sgu, b_sgu, sink, w_fourier, b_fourier):
    bsz, seq, d = x.shape
    n_ctx = ctx.shape[1]
    depth = w_mod.shape[0]

    rope = _rope_tables(seq)
    n_mod_rows = 8 * ((bsz + 1 + 7) // 8)
    cc = jnp.zeros((n_mod_rows, d), F32).at[:bsz].set(c).at[bsz].set(c_ctx)
    mods = _modulation(cc, w_mod, b_mod)

    w_in_b = w_in.astype(BF16)
    w_out_b = w_out.astype(BF16)
    w_sgu_b = w_sgu.astype(BF16)
    w_f_b = w_fourier.astype(BF16)

    tm_in = _pick_tile(seq, 1024)
    tm_in_c = _pick_tile(n_ctx, 512)
    tm_out = _pick_tile(seq, 256)
    tm_out_c = _pick_tile(n_ctx, 256)
    nq = _pick_tile(seq // CHUNK, 4)

    x2 = x.reshape(bsz * seq, d)
    ctx2 = ctx.reshape(bsz * n_ctx, d)
    for l in range(depth):
        last = l == depth - 1
        shift = mods[l, :bsz, None, 0:d]
        scale = mods[l, :bsz, None, d:2 * d]
        gate = mods[l, :bsz, None, 2 * d:3 * d]
        shift_c = mods[l, bsz:bsz + 1, None, 0:d]
        scale_c = mods[l, bsz:bsz + 1, None, d:2 * d]
        gate_c = mods[l, bsz:bsz + 1, None, 2 * d:3 * d]
        gpre_l = g_pre[l][None, :]
        gpost_l = g_post[l][None, :]
        gsgu_l = g_sgu[l][None, :]
        bias_full = jnp.repeat(b_sgu[l].T, HEAD_DIM, axis=1)
        bf_l = b_fourier[l].reshape(1, C_WIDTH)

        proj, cx = _inproj(x2, shift, scale, gpre_l, w_in_b[l], gsgu_l, rope, seq, tm_in)
        proj_c, cx_c = _inproj(ctx2, shift_c, scale_c, gpre_l, w_in_b[l], gsgu_l, None, bsz * n_ctx, tm_in_c)
        proj3 = proj.reshape(bsz, seq, IN_COLS)
        projc3 = proj_c.reshape(bsz, n_ctx, IN_COLS)

        ya = _gmlp(proj, w_sgu_b[l], bias_full, _pick_tile(seq, 512))
        yb = _win_attn(proj3, projc3, sink[l], nq).reshape(bsz * seq, B_WIDTH)
        ypos = _pos_dft_two_stage(cx.reshape(bsz, seq, C_WIDTH))
        yc = _chan_mix(ypos, proj3, w_f_b[l], bf_l, _pick_tile(seq, 512)).reshape(bsz * seq, C_WIDTH)
        x_new = _outproj(ya, yb, yc, w_out_b[l], gpost_l, gate, x2, seq, tm_out)

        if not last:
            ya_c = _gmlp(proj_c, w_sgu_b[l], bias_full, _pick_tile(n_ctx, 256))
            yb_c = _ctx_attn(projc3, sink[l]).reshape(bsz * n_ctx, B_WIDTH)
            ypos_c = _pos_dft_single(cx_c.reshape(bsz, n_ctx, C_WIDTH))
            yc_c = _chan_mix(ypos_c, projc3, w_f_b[l], bf_l, n_ctx).reshape(bsz * n_ctx, C_WIDTH)
            ctx2 = _outproj(ya_c, yb_c, yc_c, w_out_b[l], gpost_l, gate_c, ctx2, bsz * n_ctx, tm_out_c)
        x2 = x_new
    return x2.reshape(bsz, seq, d)
```

```python
import functools
import math

import numpy as np
import jax
import jax.numpy as jnp
from jax.experimental import pallas as pl
from jax.experimental.pallas import tpu as pltpu

F32 = jnp.float32
BF16 = jnp.bfloat16

EPS = 1e-6
GRID_W = 64
CHUNK = 128
A_HEADS = 4
A_WIDTH = 512
B_HEADS = 8
B_KV_HEADS = 2
B_GROUP = B_HEADS // B_KV_HEADS
HEAD_DIM = 128
B_WIDTH = 1024
B_KV_WIDTH = 256
ROPE_THETA = 10000.0
ROPE_FREQS = HEAD_DIM // 4
C_GROUPS = 4
C_DIM = 128
C_WIDTH = 512
IN_COLS = 5120
NEG_INF = -1e30
ATTN_SCALE = HEAD_DIM ** -0.5

TN = 512
COL_AU, COL_AV, COL_AG, COL_Q0, COL_Q1, COL_KV, COL_BG0, COL_BG1, COL_CX, COL_CG = range(10)
N_COL_TILES = IN_COLS // TN
FFT_S1 = 64

VMEM_LIMIT = 56 * 1024 * 1024


def _cparams(sem):
    return pltpu.CompilerParams(dimension_semantics=sem, vmem_limit_bytes=VMEM_LIMIT)


def _resident(shape):
    zeros = (0,) * len(shape)
    return pl.BlockSpec(shape, lambda *_: zeros, pipeline_mode=pl.Buffered(1))


def _silu(x):
    return x / (1.0 + jnp.exp(-x))


def _gelu_tanh(x):
    c = math.sqrt(2.0 / math.pi)
    return x * (0.5 * (1.0 + jnp.tanh(c * (x + 0.044715 * (x * x * x)))))


def _norm_modulate(x, g, scale, shift):
    ms = jnp.mean(x * x, axis=-1, keepdims=True)
    return (x * jax.lax.rsqrt(ms + EPS) * g) * (1.0 + scale) + shift


def _mod_kernel(cc_ref, w_ref, b_ref, o_ref):
    s = _silu(cc_ref[...]).astype(BF16)
    w = w_ref[...].astype(BF16)
    o_ref[...] = jnp.dot(s, w, preferred_element_type=F32) + b_ref[...]


def _modulation(cc, w_mod, b_mod):
    depth, d, n3 = w_mod.shape
    rows = cc.shape[0]
    tn = 768
    return pl.pallas_call(
        _mod_kernel,
        grid=(depth, n3 // tn),
        in_specs=[
            pl.BlockSpec((rows, d), lambda l, j: (0, 0)),
            pl.BlockSpec((None, d, tn), lambda l, j: (l, 0, j)),
            pl.BlockSpec((None, 1, tn), lambda l, j: (l, 0, j)),
        ],
        out_specs=pl.BlockSpec((None, rows, tn), lambda l, j: (l, 0, j)),
        out_shape=jax.ShapeDtypeStruct((depth, rows, n3), F32),
        compiler_params=_cparams(("parallel", "parallel")),
        name="modulation",
    )(cc, w_mod, b_mod.reshape(depth, 1, n3))


def _prenorm_kernel(x_ref, gpre_ref, scale_ref, shift_ref, h_ref):
    h_ref[...] = _norm_modulate(x_ref[...], gpre_ref[...], scale_ref[...], shift_ref[...]).astype(BF16)


def _prenorm(x2, g_pre, scale, shift, rows_per_mod, tm):
    r, d = x2.shape
    tiles_per_mod = rows_per_mod // tm
    mod_spec = pl.BlockSpec((None, 1, d), lambda i: (i // tiles_per_mod, 0, 0))
    return pl.pallas_call(
        _prenorm_kernel,
        grid=(r // tm,),
        in_specs=[pl.BlockSpec((tm, d), lambda i: (i, 0)), pl.BlockSpec((1, d), lambda i: (0, 0)),
                  mod_spec, mod_spec],
        out_specs=pl.BlockSpec((tm, d), lambda i: (i, 0)),
        out_shape=jax.ShapeDtypeStruct((r, d), BF16),
        compiler_params=_cparams(("parallel",)),
        name="prenorm",
    )(x2, g_pre, scale, shift)


def _rope_heads(t, cos, sin_a, sin_b, n_heads, scale):
    outs = []
    for h in range(n_heads):
        xh = t[:, h * HEAD_DIM:(h + 1) * HEAD_DIM]
        up = pltpu.roll(xh, HEAD_DIM - ROPE_FREQS, axis=1)
        dn = pltpu.roll(xh, ROPE_FREQS, axis=1)
        o = xh * cos + up * sin_a + dn * sin_b
        if scale != 1.0:
            o = o * scale
        outs.append(o)
    return outs


def _inproj_kernel(*refs, use_rope):
    if use_rope:
        h_ref, w_ref, gsgu_ref, cos_ref, sina_ref, sinb_ref, proj_ref, cx_ref = refs
    else:
        h_ref, w_ref, gsgu_ref, proj_ref, cx_ref = refs
    for j in range(N_COL_TILES):
        cols = slice(j * TN, (j + 1) * TN)
        acc = jnp.dot(h_ref[...], w_ref[:, cols], preferred_element_type=F32)
        if j == COL_AU:
            proj_ref[:, cols] = _gelu_tanh(acc).astype(BF16)
        elif j == COL_AV:
            v = _gelu_tanh(acc)
            ms = jnp.mean(v * v, axis=-1, keepdims=True)
            proj_ref[:, cols] = (v * jax.lax.rsqrt(ms + EPS) * gsgu_ref[...]).astype(BF16)
        elif j in (COL_AG, COL_BG0, COL_BG1, COL_CG):
            proj_ref[:, cols] = _silu(acc).astype(BF16)
        elif j in (COL_Q0, COL_Q1):
            if use_rope:
                outs = _rope_heads(acc, cos_ref[...], sina_ref[...], sinb_ref[...], TN // HEAD_DIM, ATTN_SCALE)
                for h, o in enumerate(outs):
                    proj_ref[:, j * TN + h * HEAD_DIM:j * TN + (h + 1) * HEAD_DIM] = o.astype(BF16)
            else:
                proj_ref[:, cols] = (acc * ATTN_SCALE).astype(BF16)
        elif j == COL_KV:
            if use_rope:
                outs = _rope_heads(acc[:, :B_KV_WIDTH], cos_ref[...], sina_ref[...], sinb_ref[...], B_KV_HEADS, 1.0)
                for h, o in enumerate(outs):
                    proj_ref[:, j * TN + h * HEAD_DIM:j * TN + (h + 1) * HEAD_DIM] = o.astype(BF16)
                proj_ref[:, j * TN + B_KV_WIDTH:(j + 1) * TN] = acc[:, B_KV_WIDTH:].astype(BF16)
            else:
                proj_ref[:, cols] = acc.astype(BF16)
        else:
            proj_ref[:, cols] = acc.astype(BF16)
            cx_ref[...] = acc


def _inproj(h2, w_in, g_sgu, rope, tm):
    r, d = h2.shape
    use_rope = rope is not None
    in_specs = [
        pl.BlockSpec((tm, d), lambda i: (i, 0)),
        _resident((d, IN_COLS)),
        pl.BlockSpec((1, A_WIDTH), lambda i: (0, 0)),
    ]
    args = [h2, w_in, g_sgu]
    if use_rope:
        tiles_per_seq = rope[0].shape[0] // tm
        for t in rope:
            in_specs.append(pl.BlockSpec((tm, HEAD_DIM), lambda i: (i % tiles_per_seq, 0)))
            args.append(t)
    return pl.pallas_call(
        functools.partial(_inproj_kernel, use_rope=use_rope),
        grid=(r // tm,),
        in_specs=in_specs,
        out_specs=[
            pl.BlockSpec((tm, IN_COLS), lambda i: (i, 0)),
            pl.BlockSpec((tm, C_WIDTH), lambda i: (i, 0)),
        ],
        out_shape=[
            jax.ShapeDtypeStruct((r, IN_COLS), BF16),
            jax.ShapeDtypeStruct((r, C_WIDTH), F32),
        ],
        compiler_params=_cparams(("parallel",)),
        name="inproj_rope" if use_rope else "inproj_ctx",
    )(*args)


def _gmlp_kernel(gu_ref, vn_ref, sg_ref, ws_ref, bias_ref, o_ref, *, n_chunks):
    for c in range(n_chunks):
        rows = slice(c * CHUNK, (c + 1) * CHUNK)
        for h in range(A_HEADS):
            cols = slice(h * HEAD_DIM, (h + 1) * HEAD_DIM)
            mixed = jnp.dot(ws_ref[h], vn_ref[rows, cols], preferred_element_type=F32)
            mixed = mixed + bias_ref[:, cols]
            y = gu_ref[rows, cols].astype(F32) * mixed * sg_ref[rows, cols].astype(F32)
            o_ref[rows, cols] = y.astype(BF16)


def _gmlp(proj, w_s, bias_full, tg):
    r = proj.shape[0]
    return pl.pallas_call(
        functools.partial(_gmlp_kernel, n_chunks=tg // CHUNK),
        grid=(r // tg,),
        in_specs=[
            pl.BlockSpec((tg, A_WIDTH), lambda i: (i, COL_AU)),
            pl.BlockSpec((tg, A_WIDTH), lambda i: (i, COL_AV)),
            pl.BlockSpec((tg, A_WIDTH), lambda i: (i, COL_AG)),
            pl.BlockSpec((A_HEADS, CHUNK, CHUNK), lambda i: (0, 0, 0)),
            pl.BlockSpec((CHUNK, A_WIDTH), lambda i: (0, 0)),
        ],
        out_specs=pl.BlockSpec((tg, A_WIDTH), lambda i: (i, 0)),
        out_shape=jax.ShapeDtypeStruct((r, A_WIDTH), BF16),
        compiler_params=_cparams(("parallel",)),
        name="gmlp",
    )(proj, proj, proj, w_s, bias_full)


def _dot_nt(a, b):
    return jax.lax.dot_general(a, b, (((1,), (1,)), ((), ())), preferred_element_type=F32)


def _stack_heads(q_ref, rows):
    return jnp.concatenate([q_ref[rows, g * HEAD_DIM:(g + 1) * HEAD_DIM] for g in range(B_GROUP)], axis=0)


def _sink_column(sink_ref, kv):
    head = jax.lax.broadcasted_iota(jnp.int32, (B_GROUP * CHUNK, 1), 0) // CHUNK
    col = jnp.full((B_GROUP * CHUNK, 1), sink_ref[kv * B_GROUP], F32)
    for g in range(1, B_GROUP):
        col = jnp.where(head == g, sink_ref[kv * B_GROUP + g], col)
    return col


def _softmax_pv(scores, vals, sink_col):
    mx = sink_col
    for s in scores:
        mx = jnp.maximum(mx, s.max(axis=-1, keepdims=True))
    den = jnp.exp(sink_col - mx)
    o = None
    for s, v in zip(scores, vals):
        p = jnp.exp(s - mx)
        den = den + p.sum(axis=-1, keepdims=True)
        pv = jnp.dot(p.astype(BF16), v, preferred_element_type=F32)
        o = pv if o is None else o + pv
    return o / den


def _store_heads(o, g_ref, o_ref, rows):
    for g in range(B_GROUP):
        cols = slice(g * HEAD_DIM, (g + 1) * HEAD_DIM)
        og = o[g * CHUNK:(g + 1) * CHUNK]
        o_ref[rows, cols] = (og * g_ref[rows, cols].astype(F32)).astype(BF16)


def _win_attn_kernel(sink_ref, q_ref, kp_ref, km_ref, kn_ref, vp_ref, vm_ref, vn_ref,
                     kc_ref, vc_ref, g_ref, o_ref, kwin_ref, vwin_ref, *, nq):
    i = pl.program_id(1)
    kv = pl.program_id(2)
    tq = nq * CHUNK
    kwin_ref[0:CHUNK] = kp_ref[...]
    kwin_ref[CHUNK:CHUNK + tq] = km_ref[...]
    kwin_ref[CHUNK + tq:] = kn_ref[...]
    vwin_ref[0:CHUNK] = vp_ref[...]
    vwin_ref[CHUNK:CHUNK + tq] = vm_ref[...]
    vwin_ref[CHUNK + tq:] = vn_ref[...]

    m_rows = B_GROUP * CHUNK
    qpos = jax.lax.broadcasted_iota(jnp.int32, (m_rows, 3 * CHUNK), 0) % CHUNK
    kpos = jax.lax.broadcasted_iota(jnp.int32, (m_rows, 3 * CHUNK), 1) - CHUNK
    band = jnp.abs(kpos - qpos) <= CHUNK
    band_first = band & ((kpos >= 0) | (i > 0))
    band_last = band & ((kpos < CHUNK) | (i < pl.num_programs(1) - 1))
    sink_col = _sink_column(sink_ref, kv)
    kc = kc_ref[...]
    vc = vc_ref[...]
    for a in range(nq):
        rows = slice(a * CHUNK, (a + 1) * CHUNK)
        win = slice(a * CHUNK, (a + 3) * CHUNK)
        q4 = _stack_heads(q_ref, rows)
        s_ctx = _dot_nt(q4, kc)
        s_loc = _dot_nt(q4, kwin_ref[win, :])
        if a == 0 and a == nq - 1:
            mask = band_first & band_last
        elif a == 0:
            mask = band_first
        elif a == nq - 1:
            mask = band_last
        else:
            mask = band
        s_loc = jnp.where(mask, s_loc, NEG_INF)
        o = _softmax_pv([s_ctx, s_loc], [vc, vwin_ref[win, :]], sink_col)
        _store_heads(o, g_ref, o_ref, rows)


def _win_attn(proj3, projc3, sink_l, nq):
    bsz, s, _ = proj3.shape
    n_ctx = projc3.shape[1]
    tq = nq * CHUNK
    nb = s // CHUNK
    kcol = (COL_KV * TN) // HEAD_DIM
    vcol = kcol + B_KV_HEADS
    smem = pl.BlockSpec(memory_space=pltpu.SMEM)

    def halo(col0, which):
        if which == "prev":
            return pl.BlockSpec((None, CHUNK, HEAD_DIM),
                                lambda b, i, kv: (b, jnp.maximum(i * nq - 1, 0), col0 + kv))
        if which == "next":
            return pl.BlockSpec((None, CHUNK, HEAD_DIM),
                                lambda b, i, kv: (b, jnp.minimum(i * nq + nq, nb - 1), col0 + kv))
        return pl.BlockSpec((None, tq, HEAD_DIM), lambda b, i, kv: (b, i, col0 + kv))

    return pl.pallas_call(
        functools.partial(_win_attn_kernel, nq=nq),
        grid=(bsz, s // tq, B_KV_HEADS),
        in_specs=[
            smem,
            pl.BlockSpec((None, tq, TN), lambda b, i, kv: (b, i, COL_Q0 + kv)),
            halo(kcol, "prev"), halo(kcol, "main"), halo(kcol, "next"),
            halo(vcol, "prev"), halo(vcol, "main"), halo(vcol, "next"),
            pl.BlockSpec((None, n_ctx, HEAD_DIM), lambda b, i, kv: (b, 0, kcol + kv)),
            pl.BlockSpec((None, n_ctx, HEAD_DIM), lambda b, i, kv: (b, 0, vcol + kv)),
            pl.BlockSpec((None, tq, TN), lambda b, i, kv: (b, i, COL_BG0 + kv)),
        ],
        out_specs=pl.BlockSpec((None, tq, TN), lambda b, i, kv: (b, i, kv)),
        out_shape=jax.ShapeDtypeStruct((bsz, s, B_WIDTH), BF16),
        scratch_shapes=[pltpu.VMEM((tq + 2 * CHUNK, HEAD_DIM), BF16),
                        pltpu.VMEM((tq + 2 * CHUNK, HEAD_DIM), BF16)],
        compiler_params=_cparams(("parallel", "parallel", "parallel")),
        name="win_attn",
    )(sink_l, proj3, proj3, proj3, proj3, proj3, proj3, proj3, projc3, projc3, proj3)


def _ctx_attn_kernel(sink_ref, q_ref, k_ref, v_ref, g_ref, o_ref, *, n_blocks):
    kv = pl.program_id(1)
    sink_col = _sink_column(sink_ref, kv)
    k = k_ref[...]
    v = v_ref[...]
    for a in range(n_blocks):
        rows = slice(a * CHUNK, (a + 1) * CHUNK)
        q4 = _stack_heads(q_ref, rows)
        o = _softmax_pv([_dot_nt(q4, k)], [v], sink_col)
        _store_heads(o, g_ref, o_ref, rows)


def _ctx_attn(projc3, sink_l):
    bsz, n_ctx, _ = projc3.shape
    kcol = (COL_KV * TN) // HEAD_DIM
    vcol = kcol + B_KV_HEADS
    return pl.pallas_call(
        functools.partial(_ctx_attn_kernel, n_blocks=n_ctx // CHUNK),
        grid=(bsz, B_KV_HEADS),
        in_specs=[
            pl.BlockSpec(memory_space=pltpu.SMEM),
            pl.BlockSpec((None, n_ctx, TN), lambda b, kv: (b, 0, COL_Q0 + kv)),
            pl.BlockSpec((None, n_ctx, HEAD_DIM), lambda b, kv: (b, 0, kcol + kv)),
            pl.BlockSpec((None, n_ctx, HEAD_DIM), lambda b, kv: (b, 0, vcol + kv)),
            pl.BlockSpec((None, n_ctx, TN), lambda b, kv: (b, 0, COL_BG0 + kv)),
        ],
        out_specs=pl.BlockSpec((None, n_ctx, TN), lambda b, kv: (b, 0, kv)),
        out_shape=jax.ShapeDtypeStruct((bsz, n_ctx, B_WIDTH), BF16),
        compiler_params=_cparams(("parallel", "parallel")),
        name="ctx_attn",
    )(sink_l, projc3, projc3, projc3, projc3)


def _dft_cos_sin(n, scale):
    k = np.arange(n, dtype=np.int64)
    ang = 2.0 * np.pi * ((k[:, None] * k[None, :]) % n).astype(np.float64) / n
    return np.cos(ang) * scale, np.sin(ang) * scale


def _pos_dft_kernel(x_ref, f_ref, o_ref, *, t):
    z = jnp.dot(f_ref[...], x_ref[...].astype(BF16), preferred_element_type=F32)
    o_ref[0] = z[:t].astype(BF16)
    o_ref[1] = z[t:].astype(BF16)


def _pos_dft_single(cx3):
    bsz, t, m = cx3.shape
    c, s = _dft_cos_sin(t, t ** -0.5)
    f = jnp.asarray(np.concatenate([c, -s], axis=0), dtype=F32).astype(BF16)
    return pl.pallas_call(
        functools.partial(_pos_dft_kernel, t=t),
        grid=(bsz,),
        in_specs=[
            pl.BlockSpec((None, t, m), lambda b: (b, 0, 0)),
            pl.BlockSpec((2 * t, t), lambda b: (0, 0)),
        ],
        out_specs=pl.BlockSpec((None, 2, t, m), lambda b: (b, 0, 0, 0)),
        out_shape=jax.ShapeDtypeStruct((bsz, 2, t, m), BF16),
        compiler_params=_cparams(("parallel",)),
        name="pos_dft_single",
    )(cx3, f)


def _fft_a_kernel(x_ref, f_ref, twr_ref, twi_ref, o_ref, *, s2, tb):
    reps = C_WIDTH // HEAD_DIM
    for j in range(tb):
        xj = x_ref[:, j, :].astype(BF16)
        z = jnp.dot(f_ref[...], xj, preferred_element_type=F32)
        zr, zi = z[:s2], z[s2:]
        twr = jnp.concatenate([twr_ref[j]] * reps, axis=1)
        twi = jnp.concatenate([twi_ref[j]] * reps, axis=1)
        o_ref[0, j] = (zr * twr - zi * twi).astype(BF16)
        o_ref[1, j] = (zr * twi + zi * twr).astype(BF16)


def _fft_b_kernel(z_ref, g_ref, o_ref):
    o_ref[...] = jnp.dot(g_ref[...], z_ref[...], preferred_element_type=F32).astype(BF16)


def _pos_dft_two_stage(cx3):
    bsz, s, m = cx3.shape
    s1 = FFT_S1
    s2 = s // s1
    tb = 8
    c2, sn2 = _dft_cos_sin(s2, s2 ** -0.5)
    f2 = jnp.asarray(np.concatenate([c2, -sn2], axis=0), dtype=F32).astype(BF16)
    c1, sn1 = _dft_cos_sin(s1, s1 ** -0.5)
    g = jnp.asarray(np.block([[c1, sn1], [-sn1, c1]]), dtype=F32).astype(BF16)
    t1 = np.arange(s1, dtype=np.int64)
    k2 = np.arange(s2, dtype=np.int64)
    ang = 2.0 * np.pi * ((t1[:, None] * k2[None, :]) % s).astype(np.float64) / s
    twr = jnp.asarray(np.repeat(np.cos(ang)[:, :, None], HEAD_DIM, axis=2), dtype=F32)
    twi = jnp.asarray(np.repeat(-np.sin(ang)[:, :, None], HEAD_DIM, axis=2), dtype=F32)

    x4 = cx3.reshape(bsz, s2, s1, m)
    za = pl.pallas_call(
        functools.partial(_fft_a_kernel, s2=s2, tb=tb),
        grid=(bsz, s1 // tb),
        in_specs=[
            pl.BlockSpec((None, s2, tb, m), lambda b, i: (b, 0, i, 0)),
            pl.BlockSpec((2 * s2, s2), lambda b, i: (0, 0)),
            pl.BlockSpec((tb, s2, HEAD_DIM), lambda b, i: (i, 0, 0)),
            pl.BlockSpec((tb, s2, HEAD_DIM), lambda b, i: (i, 0, 0)),
        ],
        out_specs=pl.BlockSpec((None, 2, tb, s2, m), lambda b, i: (b, 0, i, 0, 0)),
        out_shape=jax.ShapeDtypeStruct((bsz, 2, s1, s2, m), BF16),
        compiler_params=_cparams(("parallel", "parallel")),
        name="fft_stage_a",
    )(x4, f2, twr, twi)

    zt = za.reshape(bsz, 2 * s1, s2 * m)
    tc = min(4096, s2 * m)
    y = pl.pallas_call(
        _fft_b_kernel,
        grid=(bsz, (s2 * m) // tc),
        in_specs=[
            pl.BlockSpec((None, 2 * s1, tc), lambda b, i: (b, 0, i)),
            pl.BlockSpec((2 * s1, 2 * s1), lambda b, i: (0, 0)),
        ],
        out_specs=pl.BlockSpec((None, 2 * s1, tc), lambda b, i: (b, 0, i)),
        out_shape=jax.ShapeDtypeStruct((bsz, 2 * s1, s2 * m), BF16),
        compiler_params=_cparams(("parallel", "parallel")),
        name="fft_stage_b",
    )(zt, g)
    return y.reshape(bsz, 2, s, m)


def _chan_kernel(y_ref, cs_ref, wf_ref, bf_ref, g_ref, o_ref):
    for grp in range(C_GROUPS):
        cols = slice(grp * C_DIM, (grp + 1) * C_DIM)
        lhs = jnp.concatenate([y_ref[0, :, cols], y_ref[1, :, cols]], axis=1)
        re = jnp.dot(lhs, cs_ref[...], preferred_element_type=F32)
        out = jnp.dot(re.astype(BF16), wf_ref[grp], preferred_element_type=F32) + bf_ref[:, cols]
        o_ref[:, cols] = (out * g_ref[:, cols].astype(F32)).astype(BF16)


def _chan_mix(ypos, proj3, w_f, b_f, tr):
    bsz, _, t, m = ypos.shape
    c, s = _dft_cos_sin(C_DIM, C_DIM ** -0.5)
    cs = jnp.asarray(np.concatenate([c, s], axis=0), dtype=F32).astype(BF16)
    return pl.pallas_call(
        _chan_kernel,
        grid=(bsz, t // tr),
        in_specs=[
            pl.BlockSpec((None, 2, tr, m), lambda b, i: (b, 0, i, 0)),
            pl.BlockSpec((2 * C_DIM, C_DIM), lambda b, i: (0, 0)),
            pl.BlockSpec((C_GROUPS, C_DIM, C_DIM), lambda b, i: (0, 0, 0)),
            pl.BlockSpec((1, m), lambda b, i: (0, 0)),
            pl.BlockSpec((None, tr, TN), lambda b, i: (b, i, COL_CG)),
        ],
        out_specs=pl.BlockSpec((None, tr, m), lambda b, i: (b, i, 0)),
        out_shape=jax.ShapeDtypeStruct((bsz, t, m), BF16),
        compiler_params=_cparams(("parallel", "parallel")),
        name="chan_mix",
    )(ypos, cs, w_f, b_f, proj3)


def _outproj_kernel(*refs, emit_h, sub):
    if emit_h:
        (ya_ref, yb_ref, yc_ref, w_ref, gpost_ref, gate_ref, x_ref, gpre_ref, scale_ref, shift_ref,
         o_ref, h_ref) = refs
    else:
        ya_ref, yb_ref, yc_ref, w_ref, gpost_ref, gate_ref, x_ref, o_ref = refs
    tm = x_ref.shape[0]
    for s in range(tm // sub):
        rows = slice(s * sub, (s + 1) * sub)
        y = jnp.concatenate([ya_ref[rows, :], yb_ref[rows, :], yc_ref[rows, :]], axis=1)
        acc = jnp.dot(y, w_ref[...], preferred_element_type=F32)
        ms = jnp.mean(acc * acc, axis=-1, keepdims=True)
        x_new = x_ref[rows, :] + gate_ref[...] * (acc * jax.lax.rsqrt(ms + EPS) * gpost_ref[...])
        o_ref[rows, :] = x_new
        if emit_h:
            h_ref[rows, :] = _norm_modulate(x_new, gpre_ref[...], scale_ref[...], shift_ref[...]).astype(BF16)


def _outproj(ya, yb, yc, w_out, g_post, gate, x2, next_mod, rows_per_mod, tm):
    r, d = x2.shape
    tiles_per_mod = rows_per_mod // tm
    emit_h = next_mod is not None
    mod_spec = pl.BlockSpec((None, 1, d), lambda i: (i // tiles_per_mod, 0, 0))
    row_spec = pl.BlockSpec((tm, d), lambda i: (i, 0))
    vec_spec = pl.BlockSpec((1, d), lambda i: (0, 0))
    in_specs = [
        pl.BlockSpec((tm, A_WIDTH), lambda i: (i, 0)),
        pl.BlockSpec((tm, B_WIDTH), lambda i: (i, 0)),
        pl.BlockSpec((tm, C_WIDTH), lambda i: (i, 0)),
        _resident(w_out.shape),
        vec_spec, mod_spec, row_spec,
    ]
    args = [ya, yb, yc, w_out, g_post, gate, x2]
    out_specs = [row_spec]
    out_shape = [jax.ShapeDtypeStruct((r, d), F32)]
    if emit_h:
        in_specs += [vec_spec, mod_spec, mod_spec]
        args += list(next_mod)
        out_specs.append(row_spec)
        out_shape.append(jax.ShapeDtypeStruct((r, d), BF16))
    outs = pl.pallas_call(
        functools.partial(_outproj_kernel, emit_h=emit_h, sub=min(256, tm)),
        grid=(r // tm,),
        in_specs=in_specs,
        out_specs=out_specs,
        out_shape=out_shape,
        compiler_params=_cparams(("parallel",)),
        name="outproj",
    )(*args)
    return (outs[0], outs[1]) if emit_h else (outs[0], None)


def _rope_tables(seq):
    t = np.arange(seq)
    row = (t // GRID_W).astype(np.float64)
    col = (t % GRID_W).astype(np.float64)
    freqs = ROPE_THETA ** (-np.arange(ROPE_FREQS, dtype=np.float64) / ROPE_FREQS)
    ang_r = row[:, None] * freqs[None, :]
    ang_c = col[:, None] * freqs[None, :]
    ang = np.concatenate([ang_r, ang_r, ang_c, ang_c], axis=1)
    cos, sin = np.cos(ang), np.sin(ang)
    first_half = (np.arange(HEAD_DIM) % (2 * ROPE_FREQS)) < ROPE_FREQS
    sin_a = np.where(first_half[None, :], -sin, 0.0)
    sin_b = np.where(first_half[None, :], 0.0, sin)
    return tuple(jnp.asarray(a, dtype=F32) for a in (cos, sin_a, sin_b))


def _pick_tile(n, pref):
    t = min(pref, n)
    while n % t:
        t //= 2
    return t


def kernel(x, c, ctx, c_ctx, w_mod, b_mod, g_pre, g_post, w_in, w_out, g_sgu, w_sgu, b_sgu, sink, w_fourier, b_fourier):
    bsz, seq, d = x.shape
    n_ctx = ctx.shape[1]
    depth = w_mod.shape[0]

    rope = _rope_tables(seq)
    n_mod_rows = 8 * ((bsz + 1 + 7) // 8)
    cc = jnp.zeros((n_mod_rows, d), F32).at[:bsz].set(c).at[bsz].set(c_ctx)
    mods = _modulation(cc, w_mod, b_mod)

    w_in_b = w_in.astype(BF16)
    w_out_b = w_out.astype(BF16)
    w_sgu_b = w_sgu.astype(BF16)
    w_f_b = w_fourier.astype(BF16)

    tm = _pick_tile(seq, 512)
    tm_c = _pick_tile(bsz * n_ctx, 512)
    nq = _pick_tile(seq // CHUNK, 4)
    r_ctx = bsz * n_ctx

    def lat_mod(l, k):
        return mods[l, :bsz, None, k * d:(k + 1) * d]

    def ctx_mod(l, k):
        return mods[l, bsz:bsz + 1, None, k * d:(k + 1) * d]

    x2 = x.reshape(bsz * seq, d)
    ctx2 = ctx.reshape(r_ctx, d)
    h = _prenorm(x2, g_pre[0][None, :], lat_mod(0, 1), lat_mod(0, 0), seq, tm)
    h_c = _prenorm(ctx2, g_pre[0][None, :], ctx_mod(0, 1), ctx_mod(0, 0), r_ctx, tm_c)
    for l in range(depth):
        last = l == depth - 1
        gpost_l = g_post[l][None, :]
        gsgu_l = g_sgu[l][None, :]
        bias_full = jnp.repeat(b_sgu[l].T, HEAD_DIM, axis=1)
        bf_l = b_fourier[l].reshape(1, C_WIDTH)
        next_mod = None if last else (g_pre[l + 1][None, :], lat_mod(l + 1, 1), lat_mod(l + 1, 0))
        next_mod_c = None if last else (g_pre[l + 1][None, :], ctx_mod(l + 1, 1), ctx_mod(l + 1, 0))

        proj, cx = _inproj(h, w_in_b[l], gsgu_l, rope, tm)
        proj_c, cx_c = _inproj(h_c, w_in_b[l], gsgu_l, None, tm_c)
        proj3 = proj.reshape(bsz, seq, IN_COLS)
        projc3 = proj_c.reshape(bsz, n_ctx, IN_COLS)

        ya = _gmlp(proj, w_sgu_b[l], bias_full, _pick_tile(seq, 512))
        yb = _win_attn(proj3, projc3, sink[l], nq).reshape(bsz * seq, B_WIDTH)
        ypos = _pos_dft_two_stage(cx.reshape(bsz, seq, C_WIDTH))
        yc = _chan_mix(ypos, proj3, w_f_b[l], bf_l, _pick_tile(seq, 512)).reshape(bsz * seq, C_WIDTH)
        x2, h = _outproj(ya, yb, yc, w_out_b[l], gpost_l, lat_mod(l, 2), x2, next_mod, seq, tm)

        if not last:
            ya_c = _gmlp(proj_c, w_sgu_b[l], bias_full, _pick_tile(n_ctx, 256))
            yb_c = _ctx_attn(projc3, sink[l]).reshape(r_ctx, B_WIDTH)
            ypos_c = _pos_dft_single(cx_c.reshape(bsz, n_ctx, C_WIDTH))
            yc_c = _chan_mix(ypos_c, projc3, w_f_b[l], bf_l, n_ctx).reshape(r_ctx, C_WIDTH)
            ctx2, h_c = _outproj(ya_c, yb_c, yc_c, w_out_b[l], gpost_l, ctx_mod(l, 2), ctx2,
                                 next_mod_c, r_ctx, tm_c)
    return x2.reshape(bsz, seq, d)
```

```python
import functools
import math

import numpy as np
import jax
import jax.numpy as jnp
from jax.experimental import pallas as pl
from jax.experimental.pallas import tpu as pltpu

F32 = jnp.float32
BF16 = jnp.bfloat16

EPS = 1e-6
GRID_W = 64
CHUNK = 128
A_HEADS = 4
A_WIDTH = 512
B_HEADS = 8
B_KV_HEADS = 2
B_GROUP = B_HEADS // B_KV_HEADS
HEAD_DIM = 128
B_WIDTH = 1024
B_KV_WIDTH = 256
ROPE_THETA = 10000.0
ROPE_FREQS = HEAD_DIM // 4
C_GROUPS = 4
C_DIM = 128
C_WIDTH = 512
IN_COLS = 5120
NEG_INF = -1e30
ATTN_SCALE = HEAD_DIM ** -0.5

TN = 512
COL_AU, COL_AV, COL_AG, COL_Q0, COL_Q1, COL_KV, COL_BG0, COL_BG1, COL_CX, COL_CG = range(10)
N_COL_TILES = IN_COLS // TN
FFT_S1 = 64

VMEM_LIMIT = 56 * 1024 * 1024


def _cparams(sem):
    return pltpu.CompilerParams(dimension_semantics=sem, vmem_limit_bytes=VMEM_LIMIT)


def _resident(shape):
    zeros = (0,) * len(shape)
    return pl.BlockSpec(shape, lambda *_: zeros, pipeline_mode=pl.Buffered(1))


def _silu(x):
    return x / (1.0 + jnp.exp(-x))


def _gelu_tanh(x):
    c = math.sqrt(2.0 / math.pi)
    return x * (0.5 * (1.0 + jnp.tanh(c * (x + 0.044715 * (x * x * x)))))


def _norm_modulate(x, g, scale, shift):
    ms = jnp.mean(x * x, axis=-1, keepdims=True)
    return (x * jax.lax.rsqrt(ms + EPS) * g) * (1.0 + scale) + shift


def _mod_kernel(cc_ref, w_ref, b_ref, o_ref):
    s = _silu(cc_ref[...]).astype(BF16)
    w = w_ref[...].astype(BF16)
    o_ref[...] = jnp.dot(s, w, preferred_element_type=F32) + b_ref[...]


def _modulation(cc, w_mod, b_mod):
    depth, d, n3 = w_mod.shape
    rows = cc.shape[0]
    tn = 768
    return pl.pallas_call(
        _mod_kernel,
        grid=(depth, n3 // tn),
        in_specs=[
            pl.BlockSpec((rows, d), lambda l, j: (0, 0)),
            pl.BlockSpec((None, d, tn), lambda l, j: (l, 0, j)),
            pl.BlockSpec((None, 1, tn), lambda l, j: (l, 0, j)),
        ],
        out_specs=pl.BlockSpec((None, rows, tn), lambda l, j: (l, 0, j)),
        out_shape=jax.ShapeDtypeStruct((depth, rows, n3), F32),
        compiler_params=_cparams(("parallel", "parallel")),
        name="modulation",
    )(cc, w_mod, b_mod.reshape(depth, 1, n3))


def _prenorm_kernel(x_ref, gpre_ref, scale_ref, shift_ref, h_ref):
    h_ref[...] = _norm_modulate(x_ref[...], gpre_ref[...], scale_ref[...], shift_ref[...]).astype(BF16)


def _prenorm(x2, g_pre, scale, shift, rows_per_mod, tm):
    r, d = x2.shape
    tiles_per_mod = rows_per_mod // tm
    mod_spec = pl.BlockSpec((None, 1, d), lambda i: (i // tiles_per_mod, 0, 0))
    return pl.pallas_call(
        _prenorm_kernel,
        grid=(r // tm,),
        in_specs=[pl.BlockSpec((tm, d), lambda i: (i, 0)), pl.BlockSpec((1, d), lambda i: (0, 0)),
                  mod_spec, mod_spec],
        out_specs=pl.BlockSpec((tm, d), lambda i: (i, 0)),
        out_shape=jax.ShapeDtypeStruct((r, d), BF16),
        compiler_params=_cparams(("parallel",)),
        name="prenorm",
    )(x2, g_pre, scale, shift)


def _rope_heads(t, cos, sin_a, sin_b, n_heads, scale):
    outs = []
    for h in range(n_heads):
        xh = t[:, h * HEAD_DIM:(h + 1) * HEAD_DIM]
        up = pltpu.roll(xh, HEAD_DIM - ROPE_FREQS, axis=1)
        dn = pltpu.roll(xh, ROPE_FREQS, axis=1)
        o = xh * cos + up * sin_a + dn * sin_b
        if scale != 1.0:
            o = o * scale
        outs.append(o)
    return outs


def _inproj_kernel(*refs, use_rope, tiles):
    refs = list(refs)
    cx_ref = refs.pop() if COL_CX in tiles else None
    proj_ref = refs.pop()
    if use_rope:
        h_ref, w_ref, gsgu_ref, cos_ref, sina_ref, sinb_ref = refs
    else:
        h_ref, w_ref, gsgu_ref = refs
    for idx, j in enumerate(tiles):
        cols = slice(idx * TN, (idx + 1) * TN)
        acc = jnp.dot(h_ref[...], w_ref[:, cols], preferred_element_type=F32)
        if j == COL_AU:
            proj_ref[:, cols] = _gelu_tanh(acc).astype(BF16)
        elif j == COL_AV:
            v = _gelu_tanh(acc)
            ms = jnp.mean(v * v, axis=-1, keepdims=True)
            proj_ref[:, cols] = (v * jax.lax.rsqrt(ms + EPS) * gsgu_ref[...]).astype(BF16)
        elif j in (COL_AG, COL_BG0, COL_BG1, COL_CG):
            proj_ref[:, cols] = _silu(acc).astype(BF16)
        elif j in (COL_Q0, COL_Q1):
            if use_rope:
                outs = _rope_heads(acc, cos_ref[...], sina_ref[...], sinb_ref[...], TN // HEAD_DIM, ATTN_SCALE)
                for h, o in enumerate(outs):
                    proj_ref[:, idx * TN + h * HEAD_DIM:idx * TN + (h + 1) * HEAD_DIM] = o.astype(BF16)
            else:
                proj_ref[:, cols] = (acc * ATTN_SCALE).astype(BF16)
        elif j == COL_KV:
            if use_rope:
                outs = _rope_heads(acc[:, :B_KV_WIDTH], cos_ref[...], sina_ref[...], sinb_ref[...], B_KV_HEADS, 1.0)
                for h, o in enumerate(outs):
                    proj_ref[:, idx * TN + h * HEAD_DIM:idx * TN + (h + 1) * HEAD_DIM] = o.astype(BF16)
                proj_ref[:, idx * TN + B_KV_WIDTH:(idx + 1) * TN] = acc[:, B_KV_WIDTH:].astype(BF16)
            else:
                proj_ref[:, cols] = acc.astype(BF16)
        else:
            proj_ref[:, cols] = acc.astype(BF16)
            cx_ref[...] = acc


def _inproj(h2, w_in, g_sgu, rope, tm, kv_only=False):
    r, d = h2.shape
    use_rope = rope is not None
    tiles = (COL_KV,) if kv_only else tuple(range(N_COL_TILES))
    w_spec = pl.BlockSpec((d, TN), lambda i: (0, COL_KV)) if kv_only else _resident((d, IN_COLS))
    in_specs = [
        pl.BlockSpec((tm, d), lambda i: (i, 0)),
        w_spec,
        pl.BlockSpec((1, A_WIDTH), lambda i: (0, 0)),
    ]
    args = [h2, w_in, g_sgu]
    if use_rope:
        tiles_per_seq = rope[0].shape[0] // tm
        for t in rope:
            in_specs.append(pl.BlockSpec((tm, HEAD_DIM), lambda i: (i % tiles_per_seq, 0)))
            args.append(t)
    out_specs = [pl.BlockSpec((tm, len(tiles) * TN), lambda i: (i, 0))]
    out_shape = [jax.ShapeDtypeStruct((r, len(tiles) * TN), BF16)]
    if not kv_only:
        out_specs.append(pl.BlockSpec((tm, C_WIDTH), lambda i: (i, 0)))
        out_shape.append(jax.ShapeDtypeStruct((r, C_WIDTH), F32))
    outs = pl.pallas_call(
        functools.partial(_inproj_kernel, use_rope=use_rope, tiles=tiles),
        grid=(r // tm,),
        in_specs=in_specs,
        out_specs=out_specs,
        out_shape=out_shape,
        compiler_params=_cparams(("parallel",)),
        name=("inproj_rope" if use_rope else "inproj_ctx") + ("_kv" if kv_only else ""),
    )(*args)
    return (outs[0], None) if kv_only else (outs[0], outs[1])


def _gmlp_kernel(gu_ref, vn_ref, sg_ref, ws_ref, bias_ref, o_ref, *, n_chunks):
    for c in range(n_chunks):
        rows = slice(c * CHUNK, (c + 1) * CHUNK)
        for h in range(A_HEADS):
            cols = slice(h * HEAD_DIM, (h + 1) * HEAD_DIM)
            mixed = jnp.dot(ws_ref[h], vn_ref[rows, cols], preferred_element_type=F32)
            mixed = mixed + bias_ref[:, cols]
            y = gu_ref[rows, cols].astype(F32) * mixed * sg_ref[rows, cols].astype(F32)
            o_ref[rows, cols] = y.astype(BF16)


def _gmlp(proj, w_s, bias_full, tg):
    r = proj.shape[0]
    return pl.pallas_call(
        functools.partial(_gmlp_kernel, n_chunks=tg // CHUNK),
        grid=(r // tg,),
        in_specs=[
            pl.BlockSpec((tg, A_WIDTH), lambda i: (i, COL_AU)),
            pl.BlockSpec((tg, A_WIDTH), lambda i: (i, COL_AV)),
            pl.BlockSpec((tg, A_WIDTH), lambda i: (i, COL_AG)),
            pl.BlockSpec((A_HEADS, CHUNK, CHUNK), lambda i: (0, 0, 0)),
            pl.BlockSpec((CHUNK, A_WIDTH), lambda i: (0, 0)),
        ],
        out_specs=pl.BlockSpec((tg, A_WIDTH), lambda i: (i, 0)),
        out_shape=jax.ShapeDtypeStruct((r, A_WIDTH), BF16),
        compiler_params=_cparams(("parallel",)),
        name="gmlp",
    )(proj, proj, proj, w_s, bias_full)


def _dot_nt(a, b):
    return jax.lax.dot_general(a, b, (((1,), (1,)), ((), ())), preferred_element_type=F32)


def _stack_heads(q_ref, rows):
    return jnp.concatenate([q_ref[rows, g * HEAD_DIM:(g + 1) * HEAD_DIM] for g in range(B_GROUP)], axis=0)


def _sink_column(sink_ref, kv):
    head = jax.lax.broadcasted_iota(jnp.int32, (B_GROUP * CHUNK, 1), 0) // CHUNK
    col = jnp.full((B_GROUP * CHUNK, 1), sink_ref[kv * B_GROUP], F32)
    for g in range(1, B_GROUP):
        col = jnp.where(head == g, sink_ref[kv * B_GROUP + g], col)
    return col


def _softmax_pv(scores, vals, sink_col):
    mx = sink_col
    for s in scores:
        mx = jnp.maximum(mx, s.max(axis=-1, keepdims=True))
    den = jnp.exp(sink_col - mx)
    o = None
    for s, v in zip(scores, vals):
        p = jnp.exp(s - mx)
        den = den + p.sum(axis=-1, keepdims=True)
        pv = jnp.dot(p.astype(BF16), v, preferred_element_type=F32)
        o = pv if o is None else o + pv
    return o / den


def _store_heads(o, g_ref, o_ref, rows):
    for g in range(B_GROUP):
        cols = slice(g * HEAD_DIM, (g + 1) * HEAD_DIM)
        og = o[g * CHUNK:(g + 1) * CHUNK]
        o_ref[rows, cols] = (og * g_ref[rows, cols].astype(F32)).astype(BF16)


def _win_attn_kernel(sink_ref, q_ref, kp_ref, km_ref, kn_ref, vp_ref, vm_ref, vn_ref,
                     kc_ref, vc_ref, g_ref, o_ref, kwin_ref, vwin_ref, *, nq):
    i = pl.program_id(1)
    kv = pl.program_id(2)
    tq = nq * CHUNK
    kwin_ref[0:CHUNK] = kp_ref[...]
    kwin_ref[CHUNK:CHUNK + tq] = km_ref[...]
    kwin_ref[CHUNK + tq:] = kn_ref[...]
    vwin_ref[0:CHUNK] = vp_ref[...]
    vwin_ref[CHUNK:CHUNK + tq] = vm_ref[...]
    vwin_ref[CHUNK + tq:] = vn_ref[...]

    m_rows = B_GROUP * CHUNK
    qpos = jax.lax.broadcasted_iota(jnp.int32, (m_rows, 3 * CHUNK), 0) % CHUNK
    kpos = jax.lax.broadcasted_iota(jnp.int32, (m_rows, 3 * CHUNK), 1) - CHUNK
    band = jnp.abs(kpos - qpos) <= CHUNK
    band_first = band & ((kpos >= 0) | (i > 0))
    band_last = band & ((kpos < CHUNK) | (i < pl.num_programs(1) - 1))
    sink_col = _sink_column(sink_ref, kv)
    kc = kc_ref[...]
    vc = vc_ref[...]
    for a in range(nq):
        rows = slice(a * CHUNK, (a + 1) * CHUNK)
        win = slice(a * CHUNK, (a + 3) * CHUNK)
        q4 = _stack_heads(q_ref, rows)
        s_ctx = _dot_nt(q4, kc)
        s_loc = _dot_nt(q4, kwin_ref[win, :])
        if a == 0 and a == nq - 1:
            mask = band_first & band_last
        elif a == 0:
            mask = band_first
        elif a == nq - 1:
            mask = band_last
        else:
            mask = band
        s_loc = jnp.where(mask, s_loc, NEG_INF)
        o = _softmax_pv([s_ctx, s_loc], [vc, vwin_ref[win, :]], sink_col)
        _store_heads(o, g_ref, o_ref, rows)


def _win_attn(proj3, projc3, sink_l, nq, ctx_kcol):
    bsz, s, _ = proj3.shape
    n_ctx = projc3.shape[1]
    tq = nq * CHUNK
    nb = s // CHUNK
    kcol = (COL_KV * TN) // HEAD_DIM
    vcol = kcol + B_KV_HEADS
    smem = pl.BlockSpec(memory_space=pltpu.SMEM)

    def halo(col0, which):
        if which == "prev":
            return pl.BlockSpec((None, CHUNK, HEAD_DIM),
                                lambda b, i, kv: (b, jnp.maximum(i * nq - 1, 0), col0 + kv))
        if which == "next":
            return pl.BlockSpec((None, CHUNK, HEAD_DIM),
                                lambda b, i, kv: (b, jnp.minimum(i * nq + nq, nb - 1), col0 + kv))
        return pl.BlockSpec((None, tq, HEAD_DIM), lambda b, i, kv: (b, i, col0 + kv))

    return pl.pallas_call(
        functools.partial(_win_attn_kernel, nq=nq),
        grid=(bsz, s // tq, B_KV_HEADS),
        in_specs=[
            smem,
            pl.BlockSpec((None, tq, TN), lambda b, i, kv: (b, i, COL_Q0 + kv)),
            halo(kcol, "prev"), halo(kcol, "main"), halo(kcol, "next"),
            halo(vcol, "prev"), halo(vcol, "main"), halo(vcol, "next"),
            pl.BlockSpec((None, n_ctx, HEAD_DIM), lambda b, i, kv: (b, 0, ctx_kcol + kv)),
            pl.BlockSpec((None, n_ctx, HEAD_DIM), lambda b, i, kv: (b, 0, ctx_kcol + B_KV_HEADS + kv)),
            pl.BlockSpec((None, tq, TN), lambda b, i, kv: (b, i, COL_BG0 + kv)),
        ],
        out_specs=pl.BlockSpec((None, tq, TN), lambda b, i, kv: (b, i, kv)),
        out_shape=jax.ShapeDtypeStruct((bsz, s, B_WIDTH), BF16),
        scratch_shapes=[pltpu.VMEM((tq + 2 * CHUNK, HEAD_DIM), BF16),
                        pltpu.VMEM((tq + 2 * CHUNK, HEAD_DIM), BF16)],
        compiler_params=_cparams(("parallel", "parallel", "parallel")),
        name="win_attn",
    )(sink_l, proj3, proj3, proj3, proj3, proj3, proj3, proj3, projc3, projc3, proj3)


def _ctx_attn_kernel(sink_ref, q_ref, k_ref, v_ref, g_ref, o_ref, *, n_blocks):
    kv = pl.program_id(1)
    sink_col = _sink_column(sink_ref, kv)
    k = k_ref[...]
    v = v_ref[...]
    for a in range(n_blocks):
        rows = slice(a * CHUNK, (a + 1) * CHUNK)
        q4 = _stack_heads(q_ref, rows)
        o = _softmax_pv([_dot_nt(q4, k)], [v], sink_col)
        _store_heads(o, g_ref, o_ref, rows)


def _ctx_attn(projc3, sink_l):
    bsz, n_ctx, _ = projc3.shape
    kcol = (COL_KV * TN) // HEAD_DIM
    vcol = kcol + B_KV_HEADS
    return pl.pallas_call(
        functools.partial(_ctx_attn_kernel, n_blocks=n_ctx // CHUNK),
        grid=(bsz, B_KV_HEADS),
        in_specs=[
            pl.BlockSpec(memory_space=pltpu.SMEM),
            pl.BlockSpec((None, n_ctx, TN), lambda b, kv: (b, 0, COL_Q0 + kv)),
            pl.BlockSpec((None, n_ctx, HEAD_DIM), lambda b, kv: (b, 0, kcol + kv)),
            pl.BlockSpec((None, n_ctx, HEAD_DIM), lambda b, kv: (b, 0, vcol + kv)),
            pl.BlockSpec((None, n_ctx, TN), lambda b, kv: (b, 0, COL_BG0 + kv)),
        ],
        out_specs=pl.BlockSpec((None, n_ctx, TN), lambda b, kv: (b, 0, kv)),
        out_shape=jax.ShapeDtypeStruct((bsz, n_ctx, B_WIDTH), BF16),
        compiler_params=_cparams(("parallel", "parallel")),
        name="ctx_attn",
    )(sink_l, projc3, projc3, projc3, projc3)


def _dft_cos_sin(n, scale):
    k = np.arange(n, dtype=np.int64)
    ang = 2.0 * np.pi * ((k[:, None] * k[None, :]) % n).astype(np.float64) / n
    return np.cos(ang) * scale, np.sin(ang) * scale


def _pos_dft_kernel(x_ref, f_ref, o_ref, *, t):
    z = jnp.dot(f_ref[...], x_ref[...].astype(BF16), preferred_element_type=F32)
    o_ref[0] = z[:t].astype(BF16)
    o_ref[1] = z[t:].astype(BF16)


def _pos_dft_single(cx3):
    bsz, t, m = cx3.shape
    c, s = _dft_cos_sin(t, t ** -0.5)
    f = jnp.asarray(np.concatenate([c, -s], axis=0), dtype=F32).astype(BF16)
    return pl.pallas_call(
        functools.partial(_pos_dft_kernel, t=t),
        grid=(bsz,),
        in_specs=[
            pl.BlockSpec((None, t, m), lambda b: (b, 0, 0)),
            pl.BlockSpec((2 * t, t), lambda b: (0, 0)),
        ],
        out_specs=pl.BlockSpec((None, 2, t, m), lambda b: (b, 0, 0, 0)),
        out_shape=jax.ShapeDtypeStruct((bsz, 2, t, m), BF16),
        compiler_params=_cparams(("parallel",)),
        name="pos_dft_single",
    )(cx3, f)


def _fft_a_kernel(x_ref, a_ref, twr_ref, twi_ref, o_ref):
    s2, ga, m = x_ref.shape
    n = s2 * ga
    k2h, _, _, gb, _ = o_ref.shape
    x = x_ref[...].reshape(n, m).astype(BF16)
    z = jnp.dot(a_ref[...], x, preferred_element_type=F32)
    twr = twr_ref[...]
    twi = twi_ref[...]
    for grp in range(m // HEAD_DIM):
        cols = slice(grp * HEAD_DIM, (grp + 1) * HEAD_DIM)
        zr, zi = z[:n, cols], z[n:, cols]
        o_ref[:, 0, :, :, cols] = (zr * twr - zi * twi).astype(BF16).reshape(k2h, ga, gb, HEAD_DIM)
        o_ref[:, 1, :, :, cols] = (zr * twi + zi * twr).astype(BF16).reshape(k2h, ga, gb, HEAD_DIM)


def _fft_b_kernel(z_ref, b_ref, cs_ref, wf_ref, bf_ref, g_ref, o_ref):
    _, t1h, ga, gb, m = z_ref.shape
    s1 = t1h * ga
    n = s1 * gb
    z = z_ref[...].reshape(2 * n, m)
    y = jnp.dot(b_ref[...], z, preferred_element_type=F32)
    gate = g_ref[...].reshape(n, m)
    for grp in range(C_GROUPS):
        cols = slice(grp * C_DIM, (grp + 1) * C_DIM)
        lhs = jnp.concatenate([y[:n, cols], y[n:, cols]], axis=1).astype(BF16)
        re = jnp.dot(lhs, cs_ref[...], preferred_element_type=F32)
        out = jnp.dot(re.astype(BF16), wf_ref[grp], preferred_element_type=F32) + bf_ref[:, cols]
        out = out * gate[:, cols].astype(F32)
        o_ref[:, :, cols] = out.astype(BF16).reshape(s1, gb, C_DIM)


def _chan_dft_matrix():
    c, s = _dft_cos_sin(C_DIM, C_DIM ** -0.5)
    return jnp.asarray(np.concatenate([c, s], axis=0), dtype=F32).astype(BF16)


def _fourier_two_stage(cx3, proj3, w_f, b_f):
    bsz, s, m = cx3.shape
    s1 = FFT_S1
    s2 = s // s1
    ga, gb = 8, 16
    t1h, k2h = s1 // ga, s2 // gb
    na, nb = s2 * ga, s1 * gb

    c2, sn2 = _dft_cos_sin(s2, s2 ** -0.5)
    f2 = np.stack([c2, -sn2]).reshape(2, k2h, gb, s2)
    a6 = np.einsum("rhlt,ab->rhaltb", f2, np.eye(ga))
    a_mat = jnp.asarray(a6.reshape(2 * na, na), dtype=F32).astype(BF16)
    c1, sn1 = _dft_cos_sin(s1, s1 ** -0.5)
    g = np.block([[c1, sn1], [-sn1, c1]])
    b_mat = jnp.asarray(np.kron(g, np.eye(gb)), dtype=F32).astype(BF16)
    t1 = np.arange(s1, dtype=np.int64).reshape(t1h, 1, ga, 1)
    k2 = np.arange(s2, dtype=np.int64).reshape(1, k2h, 1, gb)
    ang = 2.0 * np.pi * ((t1 * k2) % s).astype(np.float64) / s
    ang = np.repeat(ang.reshape(t1h, na, 1), HEAD_DIM, axis=2)
    twr = jnp.asarray(np.cos(ang), dtype=F32)
    twi = jnp.asarray(-np.sin(ang), dtype=F32)

    x5 = cx3.reshape(bsz, s2, t1h, ga, m)
    za = pl.pallas_call(
        _fft_a_kernel,
        grid=(bsz, t1h),
        in_specs=[
            pl.BlockSpec((None, s2, None, ga, m), lambda b, i: (b, 0, i, 0, 0)),
            _resident((2 * na, na)),
            pl.BlockSpec((None, na, HEAD_DIM), lambda b, i: (i, 0, 0)),
            pl.BlockSpec((None, na, HEAD_DIM), lambda b, i: (i, 0, 0)),
        ],
        out_specs=pl.BlockSpec((None, k2h, 2, None, ga, gb, m), lambda b, i: (b, 0, 0, i, 0, 0, 0)),
        out_shape=jax.ShapeDtypeStruct((bsz, k2h, 2, t1h, ga, gb, m), BF16),
        compiler_params=_cparams(("parallel", "parallel")),
        name="fft_stage_a",
    )(x5, a_mat, twr, twi)

    proj5 = proj3.reshape(bsz, s1, k2h, gb, IN_COLS)
    yc = pl.pallas_call(
        _fft_b_kernel,
        grid=(bsz, k2h),
        in_specs=[
            pl.BlockSpec((None, None, 2, t1h, ga, gb, m), lambda b, i: (b, i, 0, 0, 0, 0, 0)),
            _resident((2 * nb, 2 * nb)),
            pl.BlockSpec((2 * C_DIM, C_DIM), lambda b, i: (0, 0)),
            pl.BlockSpec((C_GROUPS, C_DIM, C_DIM), lambda b, i: (0, 0, 0)),
            pl.BlockSpec((1, m), lambda b, i: (0, 0)),
            pl.BlockSpec((None, s1, None, gb, TN), lambda b, i: (b, 0, i, 0, COL_CG)),
        ],
        out_specs=pl.BlockSpec((None, s1, None, gb, m), lambda b, i: (b, 0, i, 0, 0)),
        out_shape=jax.ShapeDtypeStruct((bsz, s1, k2h, gb, m), BF16),
        compiler_params=_cparams(("parallel", "parallel")),
        name="fft_stage_b",
    )(za, b_mat, _chan_dft_matrix(), w_f, b_f, proj5)
    return yc.reshape(bsz, s, m)


def _chan_kernel(y_ref, cs_ref, wf_ref, bf_ref, g_ref, o_ref):
    for grp in range(C_GROUPS):
        cols = slice(grp * C_DIM, (grp + 1) * C_DIM)
        lhs = jnp.concatenate([y_ref[0, :, cols], y_ref[1, :, cols]], axis=1)
        re = jnp.dot(lhs, cs_ref[...], preferred_element_type=F32)
        out = jnp.dot(re.astype(BF16), wf_ref[grp], preferred_element_type=F32) + bf_ref[:, cols]
        o_ref[:, cols] = (out * g_ref[:, cols].astype(F32)).astype(BF16)


def _chan_mix(ypos, proj3, w_f, b_f, tr):
    bsz, _, t, m = ypos.shape
    cs = _chan_dft_matrix()
    return pl.pallas_call(
        _chan_kernel,
        grid=(bsz, t // tr),
        in_specs=[
            pl.BlockSpec((None, 2, tr, m), lambda b, i: (b, 0, i, 0)),
            pl.BlockSpec((2 * C_DIM, C_DIM), lambda b, i: (0, 0)),
            pl.BlockSpec((C_GROUPS, C_DIM, C_DIM), lambda b, i: (0, 0, 0)),
            pl.BlockSpec((1, m), lambda b, i: (0, 0)),
            pl.BlockSpec((None, tr, TN), lambda b, i: (b, i, COL_CG)),
        ],
        out_specs=pl.BlockSpec((None, tr, m), lambda b, i: (b, i, 0)),
        out_shape=jax.ShapeDtypeStruct((bsz, t, m), BF16),
        compiler_params=_cparams(("parallel", "parallel")),
        name="chan_mix",
    )(ypos, cs, w_f, b_f, proj3)


def _outproj_kernel(*refs, emit_h, sub):
    if emit_h:
        (ya_ref, yb_ref, yc_ref, w_ref, gpost_ref, gate_ref, x_ref, gpre_ref, scale_ref, shift_ref,
         o_ref, h_ref) = refs
    else:
        ya_ref, yb_ref, yc_ref, w_ref, gpost_ref, gate_ref, x_ref, o_ref = refs
    tm = x_ref.shape[0]
    for s in range(tm // sub):
        rows = slice(s * sub, (s + 1) * sub)
        y = jnp.concatenate([ya_ref[rows, :], yb_ref[rows, :], yc_ref[rows, :]], axis=1)
        acc = jnp.dot(y, w_ref[...], preferred_element_type=F32)
        ms = jnp.mean(acc * acc, axis=-1, keepdims=True)
        x_new = x_ref[rows, :] + gate_ref[...] * (acc * jax.lax.rsqrt(ms + EPS) * gpost_ref[...])
        o_ref[rows, :] = x_new
        if emit_h:
            h_ref[rows, :] = _norm_modulate(x_new, gpre_ref[...], scale_ref[...], shift_ref[...]).astype(BF16)


def _outproj(ya, yb, yc, w_out, g_post, gate, x2, next_mod, rows_per_mod, tm):
    r, d = x2.shape
    tiles_per_mod = rows_per_mod // tm
    emit_h = next_mod is not None
    mod_spec = pl.BlockSpec((None, 1, d), lambda i: (i // tiles_per_mod, 0, 0))
    row_spec = pl.BlockSpec((tm, d), lambda i: (i, 0))
    vec_spec = pl.BlockSpec((1, d), lambda i: (0, 0))
    in_specs = [
        pl.BlockSpec((tm, A_WIDTH), lambda i: (i, 0)),
        pl.BlockSpec((tm, B_WIDTH), lambda i: (i, 0)),
        pl.BlockSpec((tm, C_WIDTH), lambda i: (i, 0)),
        _resident(w_out.shape),
        vec_spec, mod_spec, row_spec,
    ]
    args = [ya, yb, yc, w_out, g_post, gate, x2]
    out_specs = [row_spec]
    out_shape = [jax.ShapeDtypeStruct((r, d), F32)]
    if emit_h:
        in_specs += [vec_spec, mod_spec, mod_spec]
        args += list(next_mod)
        out_specs.append(row_spec)
        out_shape.append(jax.ShapeDtypeStruct((r, d), BF16))
    outs = pl.pallas_call(
        functools.partial(_outproj_kernel, emit_h=emit_h, sub=min(256, tm)),
        grid=(r // tm,),
        in_specs=in_specs,
        out_specs=out_specs,
        out_shape=out_shape,
        compiler_params=_cparams(("parallel",)),
        name="outproj",
    )(*args)
    return (outs[0], outs[1]) if emit_h else (outs[0], None)


def _rope_tables(seq):
    t = np.arange(seq)
    row = (t // GRID_W).astype(np.float64)
    col = (t % GRID_W).astype(np.float64)
    freqs = ROPE_THETA ** (-np.arange(ROPE_FREQS, dtype=np.float64) / ROPE_FREQS)
    ang_r = row[:, None] * freqs[None, :]
    ang_c = col[:, None] * freqs[None, :]
    ang = np.concatenate([ang_r, ang_r, ang_c, ang_c], axis=1)
    cos, sin = np.cos(ang), np.sin(ang)
    first_half = (np.arange(HEAD_DIM) % (2 * ROPE_FREQS)) < ROPE_FREQS
    sin_a = np.where(first_half[None, :], -sin, 0.0)
    sin_b = np.where(first_half[None, :], 0.0, sin)
    return tuple(jnp.asarray(a, dtype=F32) for a in (cos, sin_a, sin_b))


def _pick_tile(n, pref):
    t = min(pref, n)
    while n % t:
        t //= 2
    return t


def kernel(x, c, ctx, c_ctx, w_mod, b_mod, g_pre, g_post, w_in, w_out, g_sgu, w_sgu, b_sgu, sink, w_fourier, b_fourier):
    bsz, seq, d = x.shape
    n_ctx = ctx.shape[1]
    depth = w_mod.shape[0]

    rope = _rope_tables(seq)
    n_mod_rows = 8 * ((bsz + 1 + 7) // 8)
    cc = jnp.zeros((n_mod_rows, d), F32).at[:bsz].set(c).at[bsz].set(c_ctx)
    mods = _modulation(cc, w_mod, b_mod)

    w_in_b = w_in.astype(BF16)
    w_out_b = w_out.astype(BF16)
    w_sgu_b = w_sgu.astype(BF16)
    w_f_b = w_fourier.astype(BF16)

    tm = _pick_tile(seq, 512)
    tm_c = _pick_tile(bsz * n_ctx, 512)
    nq = _pick_tile(seq // CHUNK, 4)
    r_ctx = bsz * n_ctx

    def lat_mod(l, k):
        return mods[l, :bsz, None, k * d:(k + 1) * d]

    def ctx_mod(l, k):
        return mods[l, bsz:bsz + 1, None, k * d:(k + 1) * d]

    x2 = x.reshape(bsz * seq, d)
    ctx2 = ctx.reshape(r_ctx, d)
    h = _prenorm(x2, g_pre[0][None, :], lat_mod(0, 1), lat_mod(0, 0), seq, tm)
    h_c = _prenorm(ctx2, g_pre[0][None, :], ctx_mod(0, 1), ctx_mod(0, 0), r_ctx, tm_c)
    for l in range(depth):
        last = l == depth - 1
        gpost_l = g_post[l][None, :]
        gsgu_l = g_sgu[l][None, :]
        bias_full = jnp.repeat(b_sgu[l].T, HEAD_DIM, axis=1)
        bf_l = b_fourier[l].reshape(1, C_WIDTH)
        next_mod = None if last else (g_pre[l + 1][None, :], lat_mod(l + 1, 1), lat_mod(l + 1, 0))
        next_mod_c = None if last else (g_pre[l + 1][None, :], ctx_mod(l + 1, 1), ctx_mod(l + 1, 0))

        proj, cx = _inproj(h, w_in_b[l], gsgu_l, rope, tm)
        proj_c, cx_c = _inproj(h_c, w_in_b[l], gsgu_l, None, tm_c, kv_only=last)
        proj3 = proj.reshape(bsz, seq, IN_COLS)
        projc3 = proj_c.reshape(bsz, n_ctx, proj_c.shape[-1])
        ctx_kcol = 0 if last else (COL_KV * TN) // HEAD_DIM

        ya = _gmlp(proj, w_sgu_b[l], bias_full, _pick_tile(seq, 2048))
        yb = _win_attn(proj3, projc3, sink[l], nq, ctx_kcol).reshape(bsz * seq, B_WIDTH)
        yc = _fourier_two_stage(cx.reshape(bsz, seq, C_WIDTH), proj3, w_f_b[l], bf_l).reshape(bsz * seq, C_WIDTH)
        x2, h = _outproj(ya, yb, yc, w_out_b[l], gpost_l, lat_mod(l, 2), x2, next_mod, seq, tm)

        if not last:
            ya_c = _gmlp(proj_c, w_sgu_b[l], bias_full, _pick_tile(n_ctx, 256))
            yb_c = _ctx_attn(projc3, sink[l]).reshape(r_ctx, B_WIDTH)
            ypos_c = _pos_dft_single(cx_c.reshape(bsz, n_ctx, C_WIDTH))
            yc_c = _chan_mix(ypos_c, projc3, w_f_b[l], bf_l, n_ctx).reshape(r_ctx, C_WIDTH)
            ctx2, h_c = _outproj(ya_c, yb_c, yc_c, w_out_b[l], gpost_l, ctx_mod(l, 2), ctx2,
                                 next_mod_c, r_ctx, tm_c)
    return x2.reshape(bsz, seq, d)
```
